```python
import jax, jax.numpy as jnp
from jax import lax
import numpy as np

D_MODEL = 1024
BATCH = 16
SEQ = 2048
DEPTH = 2

MEM_TOKENS = 256
EPS = 1e-6
NEG_INF = -1e30
RET_HEADS = 4
RET_QK_DIM = 64
RET_V_DIM = 128
RET_CHUNK = 128
RET_ROT_BASE = 10000.0
GDN_HEADS = 4
GDN_HEAD_DIM = 128
GDN_CONV = 4
GDN_CHUNK = 64
DIL_GROUPS = ((128, 1), (512, 4), (2048, 16))
DIL_HEADS = 4
DIL_HEAD_DIM = 64
DIL_ROT_DIM = DIL_HEAD_DIM // 4
ROPE_THETA = 500000.0
XATTN_HEADS = 4
XATTN_HEAD_DIM = 128
FFN_DIM = 2816
FFN_CONV = 3
N_BRANCH = 3

RET_QK_W = RET_HEADS * RET_QK_DIM
RET_V_W = RET_HEADS * RET_V_DIM
GDN_W = GDN_HEADS * GDN_HEAD_DIM
DIL_W = len(DIL_GROUPS) * DIL_HEADS * DIL_HEAD_DIM
DIL_OUT_W = DIL_HEADS * DIL_HEAD_DIM
XATTN_W = XATTN_HEADS * XATTN_HEAD_DIM
IN_SIZES = (RET_QK_W, RET_QK_W, RET_V_W, RET_V_W,
            GDN_W, GDN_W, GDN_W, GDN_HEADS, GDN_HEADS, GDN_W,
            DIL_W, DIL_W, DIL_W, N_BRANCH * D_MODEL)
IN_COLS = sum(IN_SIZES)

kernel_name = 'hybrid_ret_gdn_dilated_block'


def rms_norm(x, g):
    xf = x.astype(jnp.float32)
    y = xf * lax.rsqrt(jnp.mean(xf * xf, axis=-1, keepdims=True) + EPS)
    return (y * g.astype(jnp.float32)).astype(x.dtype)


def l2_norm(x):
    xf = x.astype(jnp.float32)
    return xf * lax.rsqrt(jnp.sum(xf * xf, axis=-1, keepdims=True) + EPS)


def causal_depthwise_conv(x, w):
    k_width, channels = w.shape
    return lax.conv_general_dilated(
        x, w[:, None, :].astype(x.dtype), window_strides=(1,), padding=((k_width - 1, 0),),
        dimension_numbers=('NWC', 'WIO', 'NWC'), feature_group_count=channels)


def rotate(x, positions, inv_freq):
    ang = positions.astype(jnp.float32)[..., None] * inv_freq
    ang = ang.reshape(ang.shape[:2] + (1,) * (x.ndim - 3) + ang.shape[-1:])
    cos = jnp.cos(ang).astype(x.dtype)
    sin = jnp.sin(ang).astype(x.dtype)
    x1, x2 = jnp.split(x, 2, axis=-1)
    return jnp.concatenate([x1 * cos - x2 * sin, x2 * cos + x1 * sin], axis=-1)


def retention_chunked(q, k, v, log_gamma):
    B, T, H, Dk = q.shape
    Dv = v.shape[-1]
    C = RET_CHUNK
    N = T // C
    f = jnp.float32
    def chunks(z):
        return z.astype(f).reshape(B, N, C, H, -1).transpose(1, 0, 3, 2, 4)
    qc, kc, vc = chunks(q), chunks(k), chunks(v)
    idx = jnp.arange(C, dtype=f)
    diff = idx[:, None] - idx[None, :]
    lg = log_gamma.astype(f)
    decay = jnp.where(diff >= 0, jnp.exp(jnp.maximum(diff, 0.0) * lg[:, None, None]), 0.0)
    intra = jnp.einsum('nbhqd,nbhkd->nbhqk', qc, kc) * decay
    intra_o = jnp.einsum('nbhqk,nbhkv->nbhqv', intra, vc)
    q_decay = jnp.exp((idx + 1.0) * lg[:, None])[:, :, None]
    k_decay = jnp.exp((C - 1.0 - idx) * lg[:, None])[:, :, None]
    chunk_decay = jnp.exp(C * lg)[:, None, None]
    def step(state, inp):
        qi, ki, vi = inp
        inter = jnp.einsum('bhqd,bhdv->bhqv', qi * q_decay, state)
        state = state * chunk_decay + jnp.einsum('bhkd,bhkv->bhdv', ki * k_decay, vi)
        return state, inter
    _, inter_o = lax.scan(step, jnp.zeros((B, H, Dk, Dv), f), (qc, kc, vc))
    o = intra_o + inter_o
    return o.transpose(1, 0, 3, 2, 4).reshape(B, T, H, Dv)


def gated_delta_chunked(q, k, v, g, beta):
    B, T, H, Dk = q.shape
    Dv = v.shape[-1]
    C = GDN_CHUNK
    N = T // C
    f = jnp.float32
    def chunks(z):
        return z.astype(f).reshape(B, N, C, H, -1).transpose(1, 0, 3, 2, 4)
    qc = chunks(q) * Dk ** -0.5
    kc, vc = chunks(k), chunks(v)
    gc = jnp.cumsum(g.astype(f).reshape(B, N, C, H).transpose(1, 0, 3, 2), axis=-1)
    bc = beta.astype(f).reshape(B, N, C, H).transpose(1, 0, 3, 2)
    tri = jnp.tril(jnp.ones((C, C), bool))
    strict = jnp.tril(jnp.ones((C, C), bool), -1)
    diff = gc[..., :, None] - gc[..., None, :]
    decay = jnp.where(tri, jnp.exp(jnp.where(tri, diff, 0.0)), 0.0)
    kb = kc * bc[..., None]
    lower = jnp.where(strict, jnp.einsum('nbhid,nbhjd->nbhij', kb, kc) * decay, 0.0)
    a_mat = lower + jnp.eye(C, dtype=f)
    rhs = jnp.concatenate([vc * bc[..., None], kb * jnp.exp(gc)[..., None]], axis=-1)
    sol = lax.linalg.triangular_solve(a_mat, rhs, left_side=True, lower=True, unit_diagonal=True)
    u, w = sol[..., :Dv], sol[..., Dv:]
    attn = jnp.where(tri, jnp.einsum('nbhid,nbhjd->nbhij', qc, kc) * decay, 0.0)
    def step(S, inp):
        qi, ki, ui, wi, gi, ai = inp
        v_new = ui - jnp.einsum('bhcd,bhdv->bhcv', wi, S)
        o = (jnp.einsum('bhcd,bhdv->bhcv', qi * jnp.exp(gi)[..., None], S)
             + jnp.einsum('bhij,bhjv->bhiv', ai, v_new))
        g_last = gi[..., -1]
        S = (S * jnp.exp(g_last)[..., None, None]
             + jnp.einsum('bhcd,bhcv->bhdv', ki * jnp.exp(g_last[..., None] - gi)[..., None], v_new))
        return S, o
    _, o = lax.scan(step, jnp.zeros((B, H, Dk, Dv), f), (qc, kc, u, w, gc, attn))
    return o.transpose(1, 0, 3, 2, 4).reshape(B, T, H, Dv)


def dilated_window_attention(q, k, v, dilation, span):
    B, T, H, Dh = q.shape
    L = T // dilation
    nb = -(-L // span)
    Lp = nb * span
    def to_blocks(z):
        z = z.reshape(B, L, dilation, H, Dh).transpose(0, 2, 1, 3, 4)
        z = jnp.pad(z, ((0, 0), (0, 0), (0, Lp - L), (0, 0), (0, 0)))
        return z.reshape(B, dilation, nb, span, H, Dh)
    qb, kb, vb = to_blocks(q), to_blocks(k), to_blocks(v)
    def with_prev(z):
        prev = jnp.pad(z, ((0, 0), (0, 0), (1, 0), (0, 0), (0, 0), (0, 0)))[:, :, :-1]
        return jnp.concatenate([prev, z], axis=3)
    kc, vc = with_prev(kb), with_prev(vb)
    s = jnp.einsum('bcnqhd,bcnkhd->bcnhqk', qb, kc).astype(jnp.float32) * Dh ** -0.5
    blk = jnp.arange(nb)[:, None, None]
    iq = jnp.arange(span)[None, :, None]
    ik = jnp.arange(2 * span)[None, None, :]
    dist = span + iq - ik
    valid = (dist >= 0) & (dist <= span) & ((blk > 0) | (ik >= span))
    s = jnp.where(valid[:, None], s, NEG_INF)
    m = jnp.max(s, axis=-1, keepdims=True)
    p = jnp.exp(s - m)
    den = jnp.sum(p, axis=-1)
    lse = jnp.swapaxes(m[..., 0] + jnp.log(den), -1, -2)
    o = jnp.einsum('bcnhqk,bcnkhd->bcnqhd', p, vc.astype(jnp.float32))
    o = o / jnp.swapaxes(den, -1, -2)[..., None]
    o = o.reshape(B, dilation, Lp, H, Dh)[:, :, :L].transpose(0, 2, 1, 3, 4).reshape(B, T, H, Dh)
    lse = lse.reshape(B, dilation, Lp, H)[:, :, :L].transpose(0, 2, 1, 3).reshape(B, T, H)
    return o, lse


def hybrid_mixer(h, positions, w_in, ret_norm_g, gdn_conv_w, gdn_a_log, gdn_dt_bias, gdn_norm_g,
                 w_br_ret, w_br_gdn, w_br_dil, w_out):
    B, T, _ = h.shape
    f = jnp.float32
    split_points = np.cumsum(IN_SIZES)[:-1].tolist()
    (r_q, r_k, r_v, r_g, g_q, g_k, g_v, g_a, g_b, g_z,
     d_q, d_k, d_v, gates) = jnp.split(h @ w_in, split_points, axis=-1)

    ret_inv = 1.0 / (RET_ROT_BASE ** jnp.linspace(0.0, 1.0, RET_QK_DIM // 2, dtype=f))
    q = rotate(r_q.reshape(B, T, RET_HEADS, RET_QK_DIM), positions, ret_inv)
    k = rotate(r_k.reshape(B, T, RET_HEADS, RET_QK_DIM), positions, ret_inv) * RET_QK_DIM ** -0.5
    log_gamma = jnp.log1p(-jnp.exp2(-5.0 - jnp.arange(RET_HEADS, dtype=f)))
    o = retention_chunked(q, k, r_v.reshape(B, T, RET_HEADS, RET_V_DIM), log_gamma)
    o = rms_norm(o, ret_norm_g) * jax.nn.silu(r_g.reshape(B, T, RET_HEADS, RET_V_DIM).astype(f))
    o_ret = o.reshape(B, T, RET_V_W).astype(h.dtype)

    qkv = jax.nn.silu(causal_depthwise_conv(jnp.concatenate([g_q, g_k, g_v], axis=-1), gdn_conv_w))
    c_q, c_k, c_v = jnp.split(qkv, 3, axis=-1)
    q = l2_norm(c_q.reshape(B, T, GDN_HEADS, GDN_HEAD_DIM))
    k = l2_norm(c_k.reshape(B, T, GDN_HEADS, GDN_HEAD_DIM))
    beta = jax.nn.sigmoid(g_b.astype(f))
    log_decay = -jnp.exp(gdn_a_log.astype(f)) * jax.nn.softplus(g_a.astype(f) + gdn_dt_bias.astype(f))
    o = gated_delta_chunked(q, k, c_v.reshape(B, T, GDN_HEADS, GDN_HEAD_DIM), log_decay, beta)
    o = rms_norm(o, gdn_norm_g) * jax.nn.silu(g_z.reshape(B, T, GDN_HEADS, GDN_HEAD_DIM).astype(f))
    o_gdn = o.reshape(B, T, GDN_W).astype(h.dtype)

    n_groups = len(DIL_GROUPS)
    dil_inv = ROPE_THETA ** (-jnp.arange(0, DIL_ROT_DIM, 2, dtype=f) / DIL_ROT_DIM)
    def partial_rope(z):
        z = z.reshape(B, T, n_groups, DIL_HEADS, DIL_HEAD_DIM)
        return jnp.concatenate([rotate(z[..., :DIL_ROT_DIM], positions, dil_inv), z[..., DIL_ROT_DIM:]], axis=-1)
    q, k = partial_rope(d_q), partial_rope(d_k)
    v = d_v.reshape(B, T, n_groups, DIL_HEADS, DIL_HEAD_DIM)
    outs, lses = [], []
    for gi, (window, dilation) in enumerate(DIL_GROUPS):
        o_g, lse_g = dilated_window_attention(q[:, :, gi], k[:, :, gi], v[:, :, gi], dilation, window // dilation)
        outs.append(o_g)
        lses.append(lse_g)
    wts = jax.nn.softmax(jnp.stack(lses, axis=2), axis=2)
    o_dil = jnp.sum(wts[..., None] * jnp.stack(outs, axis=2), axis=2).reshape(B, T, DIL_OUT_W).astype(h.dtype)

    gate = jax.nn.sigmoid(gates.reshape(B, T, N_BRANCH, D_MODEL))
    merged = (gate[:, :, 0] * (o_ret @ w_br_ret) + gate[:, :, 1] * (o_gdn @ w_br_gdn)
              + gate[:, :, 2] * (o_dil @ w_br_dil))
    return merged @ w_out


def memory_cross_attention(h, mem_n, wq, wkv, wo):
    B, T, _ = h.shape
    M = mem_n.shape[1]
    q = (h @ wq).reshape(B, T, XATTN_HEADS, XATTN_HEAD_DIM)
    k, v = jnp.split(mem_n @ wkv, 2, axis=-1)
    k = k.reshape(B, M, XATTN_HEADS, XATTN_HEAD_DIM)
    v = v.reshape(B, M, XATTN_HEADS, XATTN_HEAD_DIM)
    s = jnp.einsum('bthd,bmhd->bhtm', q, k).astype(jnp.float32) * XATTN_HEAD_DIM ** -0.5
    p = jax.nn.softmax(s, axis=-1).astype(v.dtype)
    o = jnp.einsum('bhtm,bmhd->bthd', p, v).reshape(B, T, XATTN_W)
    return o @ wo


def conv_gated_mlp(h, w_up, conv_w, conv_b, w_down):
    up = causal_depthwise_conv(h @ w_up, conv_w) + conv_b
    a, u = jnp.split(up, 2, axis=-1)
    return (jax.nn.silu(a) * u) @ w_down


def setup_inputs(seed: int = 0) -> dict:
    key = jax.random.key(seed)
    ks = iter(jax.random.split(key, 32))
    f = jnp.float32
    L = DEPTH
    def nrm(shape, fan_in):
        return jax.random.normal(next(ks), shape, f) * fan_in ** -0.5
    def gain(shape):
        return 1.0 + 0.02 * jax.random.normal(next(ks), shape, f)
    x = jax.random.normal(next(ks), (BATCH, SEQ, D_MODEL), f)
    mem = jax.random.normal(next(ks), (BATCH, MEM_TOKENS, D_MODEL), f)
    positions = (jax.random.randint(next(ks), (BATCH, 1), 0, 4096, dtype=jnp.int32)
                 + jnp.arange(SEQ, dtype=jnp.int32)[None, :])
    gdn_a_log = jnp.log(jax.random.uniform(next(ks), (L, GDN_HEADS), f, 1.0, 16.0))
    dt = jnp.exp(jax.random.uniform(next(ks), (L, GDN_HEADS), f, float(np.log(1e-3)), float(np.log(1e-1))))
    gdn_dt_bias = dt + jnp.log(-jnp.expm1(-dt))
    return {
        'x': x,
        'mem': mem,
        'positions': positions,
        'norm_mix_g': gain((L, D_MODEL)),
        'w_in': nrm((L, D_MODEL, IN_COLS), D_MODEL),
        'ret_norm_g': gain((L, RET_HEADS, RET_V_DIM)),
        'gdn_conv_w': nrm((L, GDN_CONV, 3 * GDN_W), GDN_CONV),
        'gdn_a_log': gdn_a_log,
        'gdn_dt_bias': gdn_dt_bias,
        'gdn_norm_g': gain((L, GDN_HEAD_DIM)),
        'w_br_ret': nrm((L, RET_V_W, D_MODEL), RET_V_W),
        'w_br_gdn': nrm((L, GDN_W, D_MODEL), GDN_W),
        'w_br_dil': nrm((L, DIL_OUT_W, D_MODEL), DIL_OUT_W),
        'w_out': nrm((L, D_MODEL, D_MODEL), D_MODEL),
        'norm_xattn_g': gain((L, D_MODEL)),
        'norm_mem_g': gain((L, D_MODEL)),
        'xattn_wq': nrm((L, D_MODEL, XATTN_W), D_MODEL),
        'xattn_wkv': nrm((L, D_MODEL, 2 * XATTN_W), D_MODEL),
        'xattn_wo': nrm((L, XATTN_W, D_MODEL), XATTN_W),
        'norm_ffn_g': gain((L, D_MODEL)),
        'ffn_w_up': nrm((L, D_MODEL, 2 * FFN_DIM), D_MODEL),
        'ffn_conv_w': nrm((L, FFN_CONV, 2 * FFN_DIM), FFN_CONV),
        'ffn_conv_b': 0.01 * jax.random.normal(next(ks), (L, 2 * FFN_DIM), f),
        'ffn_w_down': nrm((L, FFN_DIM, D_MODEL), FFN_DIM),
        'final_norm_g': gain((D_MODEL,)),
    }


def reference(x, mem, positions, norm_mix_g, w_in, ret_norm_g, gdn_conv_w, gdn_a_log, gdn_dt_bias,
              gdn_norm_g, w_br_ret, w_br_gdn, w_br_dil, w_out, norm_xattn_g, norm_mem_g, xattn_wq,
              xattn_wkv, xattn_wo, norm_ffn_g, ffn_w_up, ffn_conv_w, ffn_conv_b, ffn_w_down, final_norm_g):
    for l in range(DEPTH):
        x = x + hybrid_mixer(rms_norm(x, norm_mix_g[l]), positions, w_in[l], ret_norm_g[l], gdn_conv_w[l],
                             gdn_a_log[l], gdn_dt_bias[l], gdn_norm_g[l], w_br_ret[l], w_br_gdn[l],
                             w_br_dil[l], w_out[l])
        x = x + memory_cross_attention(rms_norm(x, norm_xattn_g[l]), rms_norm(mem, norm_mem_g[l]),
                                       xattn_wq[l], xattn_wkv[l], xattn_wo[l])
        x = x + conv_gated_mlp(rms_norm(x, norm_ffn_g[l]), ffn_w_up[l], ffn_conv_w[l], ffn_conv_b[l],
                               ffn_w_down[l])
    return rms_norm(x, final_norm_g)
```

```python
import functools
import math

import jax
import jax.numpy as jnp
import numpy as np
from jax import lax
from jax.experimental import pallas as pl
from jax.experimental.pallas import tpu as pltpu

F32 = jnp.float32
BF16 = jnp.bfloat16

EPS = 1e-6
NEG_INF = -1e30

RET_HEADS, RET_QK_DIM, RET_V_DIM, RET_CHUNK = 4, 64, 128, 128
RET_ROT_BASE = 10000.0
GDN_HEADS, GDN_HEAD_DIM, GDN_CONV, GDN_CHUNK = 4, 128, 4, 64
DIL_GROUPS = ((128, 1), (512, 4), (2048, 16))
DIL_HEADS, DIL_HEAD_DIM = 4, 64
DIL_ROT_DIM = DIL_HEAD_DIM // 4
DIL_SPAN = 128
ROPE_THETA = 500000.0
XATTN_HEADS, XATTN_HEAD_DIM = 4, 128
FFN_CONV = 3
N_BRANCH = 3

RET_QK_W = RET_HEADS * RET_QK_DIM
RET_V_W = RET_HEADS * RET_V_DIM
GDN_W = GDN_HEADS * GDN_HEAD_DIM
DIL_GW = DIL_HEADS * DIL_HEAD_DIM
DIL_W = len(DIL_GROUPS) * DIL_GW
XATTN_W = XATTN_HEADS * XATTN_HEAD_DIM

LANES = 128
SUBLANES = 8
VMEM_LIMIT_BYTES = 56 * 1024 * 1024

COL_GATES = 0
COL_RQ = COL_GATES + N_BRANCH * 1024
COL_RK = COL_RQ + RET_QK_W
COL_RV = COL_RK + RET_QK_W
COL_RG = COL_RV + RET_V_W
COL_GQKV = COL_RG + RET_V_W
COL_GZ = COL_GQKV + 3 * GDN_W
COL_DQ = COL_GZ + GDN_W
COL_DK = COL_DQ + DIL_W
COL_DV = COL_DK + DIL_W
PROJ_W = COL_DV + DIL_W
GAB_PAD = LANES


def _params(*sem):
    return pltpu.CompilerParams(dimension_semantics=sem, vmem_limit_bytes=VMEM_LIMIT_BYTES)


def _rms(x, g):
    return x * lax.rsqrt(jnp.mean(x * x, axis=-1, keepdims=True) + EPS) * g


def _dot(a, b):
    return jnp.dot(a, b, preferred_element_type=F32)


def _dot_nt(a, b):
    return lax.dot_general(a, b, (((1,), (1,)), ((), ())), preferred_element_type=F32)


def _dot_tn(a, b):
    return lax.dot_general(a, b, (((0,), (0,)), ((), ())), preferred_element_type=F32)


def _silu(x):
    return x * jax.nn.sigmoid(x)


def _in_proj_kernel(x_ref, g_ref, w_ref, wab_ref, wabt_ref, proj_ref, gab_ref, gabt_ref, h_scr):
    @pl.when(pl.program_id(1) == 0)
    def _():
        hb = _rms(x_ref[...], g_ref[...]).astype(BF16)
        h_scr[...] = hb
        gab_ref[...] = _dot(hb, wab_ref[...])
        gabt_ref[...] = _dot_nt(wabt_ref[...], hb)

    proj_ref[...] = _dot(h_scr[...], w_ref[...]).astype(BF16)


def _in_proj(x2, g, w_main, w_ab, w_abt, *, tm, tn):
    m, d = x2.shape
    n = w_main.shape[1]
    return pl.pallas_call(
        _in_proj_kernel,
        grid=(m // tm, n // tn),
        in_specs=[
            pl.BlockSpec((tm, d), lambda i, j: (i, 0)),
            pl.BlockSpec((1, d), lambda i, j: (0, 0)),
            pl.BlockSpec((d, tn), lambda i, j: (0, j)),
            pl.BlockSpec((d, GAB_PAD), lambda i, j: (0, 0)),
            pl.BlockSpec((SUBLANES, d), lambda i, j: (0, 0)),
        ],
        out_specs=[
            pl.BlockSpec((tm, tn), lambda i, j: (i, j)),
            pl.BlockSpec((tm, GAB_PAD), lambda i, j: (i, 0)),
            pl.BlockSpec((SUBLANES, tm), lambda i, j: (0, i)),
        ],
        out_shape=[
            jax.ShapeDtypeStruct((m, n), BF16),
            jax.ShapeDtypeStruct((m, GAB_PAD), F32),
            jax.ShapeDtypeStruct((SUBLANES, m), F32),
        ],
        scratch_shapes=[pltpu.VMEM((tm, d), BF16)],
        compiler_params=_params("parallel", "arbitrary"),
        name="in_proj",
    )(x2, g, w_main, w_ab, w_abt)


def _norm_matmul_kernel(x_ref, g_ref, w_ref, o_ref):
    hb = _rms(x_ref[...], g_ref[...]).astype(BF16)
    o_ref[...] = _dot(hb, w_ref[...]).astype(BF16)


def _norm_matmul(x2, g, w, *, tm):
    m, d = x2.shape
    n = w.shape[1]
    return pl.pallas_call(
        _norm_matmul_kernel,
        grid=(m // tm,),
        in_specs=[
            pl.BlockSpec((tm, d), lambda i: (i, 0)),
            pl.BlockSpec((1, d), lambda i: (0, 0)),
            pl.BlockSpec((d, n), lambda i: (0, 0)),
        ],
        out_specs=pl.BlockSpec((tm, n), lambda i: (i, 0)),
        out_shape=jax.ShapeDtypeStruct((m, n), BF16),
        compiler_params=_params("parallel"),
        name="norm_matmul",
    )(x2, g, w)


def _ret_kernel(pos_ref, inv_ref, q_ref, k_ref, v_ref, gate_ref, ng_ref, o_ref, state):
    t = q_ref.shape[1]
    c = RET_CHUNK
    state[...] = jnp.zeros_like(state)

    lane = lax.broadcasted_iota(jnp.int32, (1, RET_QK_W), 1)
    first_half = (lane % RET_QK_DIM) < (RET_QK_DIM // 2)
    head_of_lane = lane // RET_QK_DIM
    row_i = lax.broadcasted_iota(jnp.int32, (c, c), 0)
    col_j = lax.broadcasted_iota(jnp.int32, (c, c), 1)
    diff = (row_i - col_j).astype(F32)
    idx_col = lax.broadcasted_iota(jnp.int32, (c, 1), 0).astype(F32)
    idx_row = lax.broadcasted_iota(jnp.int32, (1, c), 1).astype(F32)
    log_gamma = [math.log1p(-(2.0 ** (-5.0 - h))) for h in range(RET_HEADS)]

    def rope(z, cs, sn):
        partner = jnp.where(first_half, -pltpu.roll(z, RET_QK_W - RET_QK_DIM // 2, 1),
                            pltpu.roll(z, RET_QK_DIM // 2, 1))
        return z * cs + partner * sn

    def chunk(ci, carry):
        r0 = pl.multiple_of(ci * c, c)
        rows = pl.ds(r0, c)
        ang = pos_ref[0, rows, :] * inv_ref[...]
        cs, sn = jnp.cos(ang), jnp.sin(ang)
        q = rope(q_ref[0, rows, :].astype(F32), cs, sn)
        k = rope(k_ref[0, rows, :].astype(F32), cs, sn) * (RET_QK_DIM ** -0.5)
        k_t = k.T
        k_t_bf = k_t.astype(BF16)
        for h in range(RET_HEADS):
            lg = log_gamma[h]
            cols = slice(h * RET_V_DIM, (h + 1) * RET_V_DIM)
            head_mask = head_of_lane == h
            decay = jnp.where(diff >= 0, jnp.exp(jnp.maximum(diff, 0.0) * lg), 0.0)
            q_decay = jnp.exp((idx_col + 1.0) * lg)
            k_decay = jnp.exp((c - 1.0 - idx_row) * lg)
            chunk_decay = math.exp(c * lg)
            qm = jnp.where(head_mask, q, 0.0)
            v = v_ref[0, rows, cols]
            s = _dot(qm.astype(BF16), k_t_bf) * decay
            st = state[h]
            o = _dot(s.astype(BF16), v) + _dot((qm * q_decay).astype(BF16), st.astype(BF16))
            state[h] = st * chunk_decay + _dot((k_t * k_decay).astype(BF16), v)
            y = _rms(o, ng_ref[h:h + 1, :])
            y = y * _silu(gate_ref[0, rows, cols].astype(F32))
            o_ref[0, rows, cols] = y.astype(BF16)
        return carry

    lax.fori_loop(0, t // c, chunk, 0)


def _retention(proj3, pos3, inv_row, norm_g):
    b, t, _ = proj3.shape
    return pl.pallas_call(
        _ret_kernel,
        grid=(b,),
        in_specs=[
            pl.BlockSpec((1, t, 1), lambda i: (i, 0, 0)),
            pl.BlockSpec((1, RET_QK_W), lambda i: (0, 0)),
            pl.BlockSpec((1, t, RET_QK_W), lambda i: (i, 0, COL_RQ // RET_QK_W)),
            pl.BlockSpec((1, t, RET_QK_W), lambda i: (i, 0, COL_RK // RET_QK_W)),
            pl.BlockSpec((1, t, RET_V_W), lambda i: (i, 0, COL_RV // RET_V_W)),
            pl.BlockSpec((1, t, RET_V_W), lambda i: (i, 0, COL_RG // RET_V_W)),
            pl.BlockSpec((RET_HEADS, RET_V_DIM), lambda i: (0, 0)),
        ],
        out_specs=pl.BlockSpec((1, t, RET_V_W), lambda i: (i, 0, 0)),
        out_shape=jax.ShapeDtypeStruct((b, t, RET_V_W), BF16),
        scratch_shapes=[pltpu.VMEM((RET_HEADS, RET_QK_W, RET_V_DIM), F32)],
        compiler_params=_params("parallel"),
        name="retention",
    )(pos3, inv_row, proj3, proj3, proj3, proj3, norm_g)


def _softplus(x):
    return jnp.maximum(x, 0.0) + jnp.log1p(jnp.exp(-jnp.abs(x)))


def _gdn_kernel(qkv_ref, z_ref, gab_ref, gabt_ref, cw_ref, alog_l_ref, dt_l_ref, alog_s_ref,
                dt_s_ref, ng_ref, o_ref, xpad, state):
    t = qkv_ref.shape[1]
    c = GDN_CHUNK
    hd = GDN_HEAD_DIM
    halo = SUBLANES
    xpad[0:halo, :] = jnp.zeros((halo, xpad.shape[1]), F32)
    xpad[halo:, :] = qkv_ref[0].astype(F32)
    state[...] = jnp.zeros_like(state)

    row_i = lax.broadcasted_iota(jnp.int32, (c, c), 0)
    col_j = lax.broadcasted_iota(jnp.int32, (c, c), 1)
    tri = row_i >= col_j
    strict = row_i > col_j
    tril_ones = tri.astype(F32)
    triu_ones = (row_i <= col_j).astype(F32)
    scale = hd ** -0.5
    n_doublings = int(math.log2(c)) - 1

    def chunk(ci, carry):
        r0 = pl.multiple_of(ci * c, c)
        rows = pl.ds(r0, c)
        win = xpad[pl.ds(r0, c + halo), :]
        y = cw_ref[0:1, :] * win[halo - 3:halo - 3 + c, :]
        for j in range(1, GDN_CONV):
            off = halo - (GDN_CONV - 1) + j
            y = y + cw_ref[j:j + 1, :] * win[off:off + c, :]
        y = _silu(y)

        gab = gab_ref[rows, :]
        g_col = -jnp.exp(alog_l_ref[...]) * _softplus(gab + dt_l_ref[...])
        gc_col = jnp.dot(tril_ones, g_col, precision=lax.Precision.HIGHEST,
                         preferred_element_type=F32)
        beta_col = jax.nn.sigmoid(gab)
        gabt = gabt_ref[ci]
        g_row = -jnp.exp(alog_s_ref[...]) * _softplus(gabt + dt_s_ref[...])
        gc_row = jnp.dot(g_row, triu_ones, precision=lax.Precision.HIGHEST,
                         preferred_element_type=F32)

        for h in range(GDN_HEADS):
            cols = slice(h * hd, (h + 1) * hd)
            q = y[:, h * hd:(h + 1) * hd]
            k = y[:, GDN_W + h * hd:GDN_W + (h + 1) * hd]
            v = y[:, 2 * GDN_W + h * hd:2 * GDN_W + (h + 1) * hd]
            q = q * lax.rsqrt(jnp.sum(q * q, axis=-1, keepdims=True) + EPS)
            k = k * lax.rsqrt(jnp.sum(k * k, axis=-1, keepdims=True) + EPS)
            gcc = gc_col[:, h:h + 1]
            gcr = gc_row[h:h + 1, :]
            beta = beta_col[:, GDN_HEADS + h:GDN_HEADS + h + 1]
            decay = jnp.where(tri, jnp.exp(jnp.where(tri, gcc - gcr, 0.0)), 0.0)
            kb = k * beta
            k_bf = k.astype(BF16)
            qs = q * scale
            n_mat = jnp.where(strict, _dot_nt(kb.astype(BF16), k_bf) * decay, 0.0)
            attn = jnp.where(tri, _dot_nt(qs.astype(BF16), k_bf) * decay, 0.0)
            eg = jnp.exp(gcc)
            x = jnp.concatenate([v * beta, kb * eg], axis=1)
            x = x - _dot(n_mat.astype(BF16), x.astype(BF16))
            m_pow = n_mat
            for _ in range(n_doublings):
                m_bf = m_pow.astype(BF16)
                m_pow = _dot(m_bf, m_bf)
                x = x + _dot(m_pow.astype(BF16), x.astype(BF16))
            u, w = x[:, :hd], x[:, hd:]
            st = state[h]
            st_bf = st.astype(BF16)
            v_new = u - _dot(w.astype(BF16), st_bf)
            o = _dot((qs * eg).astype(BF16), st_bf) + _dot(attn.astype(BF16), v_new.astype(BF16))
            g_last = gcc[c - 1:c, :]
            kg = k * jnp.exp(g_last - gcc)
            state[h] = st * jnp.exp(g_last) + _dot_tn(kg.astype(BF16), v_new.astype(BF16))
            out = _rms(o, ng_ref[...]) * _silu(z_ref[0, rows, cols].astype(F32))
            o_ref[0, rows, cols] = out.astype(BF16)
        return carry

    lax.fori_loop(0, t // c, chunk, 0)


def _gdn(proj3, gab, gabt, conv_w, alog_l, dt_l, alog_s, dt_s, norm_g):
    b, t, _ = proj3.shape
    return pl.pallas_call(
        _gdn_kernel,
        grid=(b,),
        in_specs=[
            pl.BlockSpec((1, t, 3 * GDN_W), lambda i: (i, 0, COL_GQKV // (3 * GDN_W))),
            pl.BlockSpec((1, t, GDN_W), lambda i: (i, 0, COL_GZ // GDN_W)),
            pl.BlockSpec((t, GAB_PAD), lambda i: (i, 0)),
            pl.BlockSpec((t // GDN_CHUNK, SUBLANES, GDN_CHUNK), lambda i: (i, 0, 0)),
            pl.BlockSpec((GDN_CONV, 3 * GDN_W), lambda i: (0, 0)),
            pl.BlockSpec((1, GAB_PAD), lambda i: (0, 0)),
            pl.BlockSpec((1, GAB_PAD), lambda i: (0, 0)),
            pl.BlockSpec((SUBLANES, 1), lambda i: (0, 0)),
            pl.BlockSpec((SUBLANES, 1), lambda i: (0, 0)),
            pl.BlockSpec((1, GDN_HEAD_DIM), lambda i: (0, 0)),
        ],
        out_specs=pl.BlockSpec((1, t, GDN_W), lambda i: (i, 0, 0)),
        out_shape=jax.ShapeDtypeStruct((b, t, GDN_W), BF16),
        scratch_shapes=[
            pltpu.VMEM((t + SUBLANES, 3 * GDN_W), F32),
            pltpu.VMEM((GDN_HEADS, GDN_HEAD_DIM, GDN_HEAD_DIM), F32),
        ],
        compiler_params=_params("parallel"),
        name="gdn",
    )(proj3, proj3, gab, gabt, conv_w, alog_l, dt_l, alog_s, dt_s, norm_g)


def _split_store(ref, lead, rows, val):
    for half in range(val.shape[1] // LANES):
        ref[lead + (half, rows, slice(None))] = val[:, half * LANES:(half + 1) * LANES]


def _split_load(ref, lead, rows):
    return jnp.concatenate([ref[lead + (half, rows, slice(None))] for half in range(DIL_GW // LANES)],
                           axis=1)


def _dil_group(g, dilation, qs, ks, vs, o_scr, lse_scr):
    t = qs.shape[1]
    span = DIL_SPAN
    nb = t // dilation // span
    lane = lax.broadcasted_iota(jnp.int32, (1, DIL_GW), 1)
    head_of_lane = lane // DIL_HEAD_DIM
    iq = lax.broadcasted_iota(jnp.int32, (span, 2 * span), 0)
    ik = lax.broadcasted_iota(jnp.int32, (span, 2 * span), 1)
    dist = span + iq - ik
    band = (dist >= 0) & (dist <= span)

    def rows_of(r, n):
        if dilation == 1:
            return pl.ds(pl.multiple_of(span * n, span), span)
        return pl.ds(r + dilation * span * n, span, stride=dilation)

    def block(r, n):
        cur = rows_of(r, n)
        prev = rows_of(r, jnp.maximum(n - 1, 0) if dilation == 1 else max(n - 1, 0))
        qb = _split_load(qs, (), cur)
        kk = jnp.concatenate([_split_load(ks, (), prev), _split_load(ks, (), cur)],
                             axis=0).astype(BF16)
        vv = jnp.concatenate([_split_load(vs, (), prev), _split_load(vs, (), cur)],
                             axis=0).astype(BF16)
        valid = band & (ik >= jnp.where(n > 0, 0, span))
        o_acc = jnp.zeros((span, DIL_GW), F32)
        lse_acc = jnp.zeros((span, DIL_GW), F32)
        for h in range(DIL_HEADS):
            head_mask = head_of_lane == h
            qm = jnp.where(head_mask, qb, 0.0).astype(BF16)
            s = jnp.where(valid, _dot_nt(qm, kk), NEG_INF)
            m = jnp.max(s, axis=-1, keepdims=True)
            p = jnp.exp(s - m)
            den = jnp.sum(p, axis=-1, keepdims=True)
            oh = _dot(p.astype(BF16), vv) / den
            o_acc = jnp.where(head_mask, oh, o_acc)
            lse_acc = jnp.where(head_mask, m + jnp.log(den), lse_acc)
        _split_store(o_scr, (g,), cur, o_acc)
        _split_store(lse_scr, (g,), cur, lse_acc)

    if dilation == 1:
        def body(n, carry):
            block(0, n)
            return carry
        lax.fori_loop(0, nb, body, 0)
    else:
        for r in range(dilation):
            for n in range(nb):
                block(r, n)


def _dil_kernel(pos_ref, inv_ref, q_ref, k_ref, v_ref, o_ref, cs_scr, s1_scr, s2_scr, qs, ks, vs,
                o_scr, lse_scr):
    g = pl.program_id(1)
    half = DIL_ROT_DIM // 2

    @pl.when(g == 0)
    def _():
        lane = lax.broadcasted_iota(jnp.int32, (1, DIL_GW), 1) % DIL_HEAD_DIM
        ang = pos_ref[0] * inv_ref[...]
        sn = jnp.sin(ang)
        cs_scr[...] = jnp.cos(ang)
        s1_scr[...] = jnp.where(lane < half, -sn, 0.0)
        s2_scr[...] = jnp.where((lane >= half) & (lane < DIL_ROT_DIM), sn, 0.0)

    def rope(z):
        return (z * cs_scr[...] + pltpu.roll(z, DIL_GW - half, 1) * s1_scr[...]
                + pltpu.roll(z, half, 1) * s2_scr[...])

    everything = slice(None)
    _split_store(qs, (), everything, rope(q_ref[0].astype(F32)) * (DIL_HEAD_DIM ** -0.5))
    _split_store(ks, (), everything, rope(k_ref[0].astype(F32)))
    _split_store(vs, (), everything, v_ref[0].astype(F32))

    for gi, (window, dilation) in enumerate(DIL_GROUPS):
        assert window // dilation == DIL_SPAN

        @pl.when(g == gi)
        def _(gi=gi, dilation=dilation):
            _dil_group(gi, dilation, qs, ks, vs, o_scr, lse_scr)

    @pl.when(g == len(DIL_GROUPS) - 1)
    def _():
        for half in range(DIL_GW // LANES):
            lses = [lse_scr[i, half] for i in range(len(DIL_GROUPS))]
            m = functools.reduce(jnp.maximum, lses)
            es = [jnp.exp(l - m) for l in lses]
            den = functools.reduce(lambda a, b: a + b, es)
            num = functools.reduce(lambda a, b: a + b,
                                   [e * o_scr[i, half] for i, e in enumerate(es)])
            o_ref[0, :, half * LANES:(half + 1) * LANES] = (num / den).astype(BF16)


def _dilated(proj3, pos3, inv_row):
    b, t, _ = proj3.shape
    ng = len(DIL_GROUPS)
    tile = pltpu.VMEM((t, DIL_GW), F32)
    split = pltpu.VMEM((DIL_GW // LANES, t, LANES), F32)
    return pl.pallas_call(
        _dil_kernel,
        grid=(b, ng),
        in_specs=[
            pl.BlockSpec((1, t, 1), lambda i, g: (i, 0, 0)),
            pl.BlockSpec((1, DIL_GW), lambda i, g: (0, 0)),
            pl.BlockSpec((1, t, DIL_GW), lambda i, g: (i, 0, COL_DQ // DIL_GW + g)),
            pl.BlockSpec((1, t, DIL_GW), lambda i, g: (i, 0, COL_DK // DIL_GW + g)),
            pl.BlockSpec((1, t, DIL_GW), lambda i, g: (i, 0, COL_DV // DIL_GW + g)),
        ],
        out_specs=pl.BlockSpec((1, t, DIL_GW), lambda i, g: (i, 0, 0)),
        out_shape=jax.ShapeDtypeStruct((b, t, DIL_GW), BF16),
        scratch_shapes=[tile, tile, tile, split, split, split,
                        pltpu.VMEM((ng, DIL_GW // LANES, t, LANES), F32),
                        pltpu.VMEM((ng, DIL_GW // LANES, t, LANES), F32)],
        compiler_params=_params("parallel", "arbitrary"),
        name="dilated",
    )(pos3, inv_row, proj3, proj3, proj3)


def _merge_kernel(x_ref, gates_ref, oret_ref, ogdn_ref, odil_ref, wr_ref, wg_ref, wd_ref, wo_ref,
                  out_ref):
    d = x_ref.shape[1]
    branches = (_dot(oret_ref[...], wr_ref[...]), _dot(ogdn_ref[...], wg_ref[...]),
                _dot(odil_ref[...], wd_ref[...]))
    merged = None
    for bi, br in enumerate(branches):
        term = jax.nn.sigmoid(gates_ref[:, bi * d:(bi + 1) * d].astype(F32)) * br
        merged = term if merged is None else merged + term
    out_ref[...] = x_ref[...] + _dot(merged.astype(BF16), wo_ref[...])


def _merge(x2, proj2, o_ret, o_gdn, o_dil, w_ret, w_gdn, w_dil, w_out, *, tm):
    m, d = x2.shape
    full = lambda a: pl.BlockSpec(a.shape, lambda i: (0, 0))
    return pl.pallas_call(
        _merge_kernel,
        grid=(m // tm,),
        in_specs=[
            pl.BlockSpec((tm, d), lambda i: (i, 0)),
            pl.BlockSpec((tm, N_BRANCH * d), lambda i: (i, COL_GATES // (N_BRANCH * d))),
            pl.BlockSpec((tm, RET_V_W), lambda i: (i, 0)),
            pl.BlockSpec((tm, GDN_W), lambda i: (i, 0)),
            pl.BlockSpec((tm, DIL_GW), lambda i: (i, 0)),
            full(w_ret), full(w_gdn), full(w_dil), full(w_out),
        ],
        out_specs=pl.BlockSpec((tm, d), lambda i: (i, 0)),
        out_shape=jax.ShapeDtypeStruct((m, d), F32),
        compiler_params=_params("parallel"),
        name="merge",
    )(x2, proj2, o_ret, o_gdn, o_dil, w_ret, w_gdn, w_dil, w_out)


def _xattn_kernel(x_ref, g_ref, kv_ref, wq_ref, wo_ref, out_ref):
    x = x_ref[0]
    hb = _rms(x, g_ref[...]).astype(BF16)
    q = _dot(hb, wq_ref[...])
    outs = []
    for h in range(XATTN_HEADS):
        cols = slice(h * XATTN_HEAD_DIM, (h + 1) * XATTN_HEAD_DIM)
        kh = kv_ref[0, :, cols]
        vh = kv_ref[0, :, XATTN_W + h * XATTN_HEAD_DIM:XATTN_W + (h + 1) * XATTN_HEAD_DIM]
        s = _dot_nt(q[:, cols].astype(BF16), kh) * (XATTN_HEAD_DIM ** -0.5)
        e = jnp.exp(s - jnp.max(s, axis=-1, keepdims=True))
        p = e / jnp.sum(e, axis=-1, keepdims=True)
        outs.append(_dot(p.astype(BF16), vh))
    o = jnp.concatenate(outs, axis=1).astype(BF16)
    out_ref[0] = x + _dot(o, wo_ref[...])


def _xattn(x3, g, kv3, wq, wo, *, tm):
    b, t, d = x3.shape
    mem = kv3.shape[1]
    return pl.pallas_call(
        _xattn_kernel,
        grid=(b, t // tm),
        in_specs=[
            pl.BlockSpec((1, tm, d), lambda i, j: (i, j, 0)),
            pl.BlockSpec((1, d), lambda i, j: (0, 0)),
            pl.BlockSpec((1, mem, 2 * XATTN_W), lambda i, j: (i, 0, 0)),
            pl.BlockSpec(wq.shape, lambda i, j: (0, 0)),
            pl.BlockSpec(wo.shape, lambda i, j: (0, 0)),
        ],
        out_specs=pl.BlockSpec((1, tm, d), lambda i, j: (i, j, 0)),
        out_shape=jax.ShapeDtypeStruct((b, t, d), F32),
        compiler_params=_params("parallel", "parallel"),
        name="xattn",
    )(x3, g, kv3, wq, wo)


def _ffn_kernel(x_ref, g_ref, wa_ref, wu_ref, cwa_ref, cwu_ref, ba_ref, bu_ref, wd_ref, fg_ref,
                out_ref, h_scr, a_scr, u_scr, *, final_norm):
    f = pl.program_id(1)
    t = x_ref.shape[1]
    halo = SUBLANES

    @pl.when(f == 0)
    def _():
        x = x_ref[0]
        h_scr[...] = _rms(x, g_ref[...]).astype(BF16)
        out_ref[0] = x
        a_scr[0:halo, :] = jnp.zeros((halo, a_scr.shape[1]), F32)
        u_scr[0:halo, :] = jnp.zeros((halo, u_scr.shape[1]), F32)

    hb = h_scr[...]
    a_scr[halo:, :] = _dot(hb, wa_ref[...])
    u_scr[halo:, :] = _dot(hb, wu_ref[...])

    def conv(scr, cw_ref, b_ref):
        acc = b_ref[...]
        for j in range(FFN_CONV):
            off = halo - (FFN_CONV - 1) + j
            acc = acc + cw_ref[j:j + 1, :] * scr[off:off + t, :]
        return acc

    act = _silu(conv(a_scr, cwa_ref, ba_ref)) * conv(u_scr, cwu_ref, bu_ref)
    out_ref[0] += _dot(act.astype(BF16), wd_ref[...])

    if final_norm:
        @pl.when(f == pl.num_programs(1) - 1)
        def _():
            out_ref[0] = _rms(out_ref[0], fg_ref[...])


def _ffn(x3, g, w_up, conv_w, conv_b, w_down, final_g, *, tf, final_norm):
    b, t, d = x3.shape
    ffn_dim = w_down.shape[0]
    nf = ffn_dim // tf
    return pl.pallas_call(
        functools.partial(_ffn_kernel, final_norm=final_norm),
        grid=(b, nf),
        in_specs=[
            pl.BlockSpec((1, t, d), lambda i, f: (i, 0, 0)),
            pl.BlockSpec((1, d), lambda i, f: (0, 0)),
            pl.BlockSpec((d, tf), lambda i, f: (0, f)),
            pl.BlockSpec((d, tf), lambda i, f: (0, nf + f)),
            pl.BlockSpec((FFN_CONV, tf), lambda i, f: (0, f)),
            pl.BlockSpec((FFN_CONV, tf), lambda i, f: (0, nf + f)),
            pl.BlockSpec((1, tf), lambda i, f: (0, f)),
            pl.BlockSpec((1, tf), lambda i, f: (0, nf + f)),
            pl.BlockSpec((tf, d), lambda i, f: (f, 0)),
            pl.BlockSpec((1, d), lambda i, f: (0, 0)),
        ],
        out_specs=pl.BlockSpec((1, t, d), lambda i, f: (i, 0, 0)),
        out_shape=jax.ShapeDtypeStruct((b, t, d), F32),
        scratch_shapes=[
            pltpu.VMEM((t, d), BF16),
            pltpu.VMEM((t + SUBLANES, tf), F32),
            pltpu.VMEM((t + SUBLANES, tf), F32),
        ],
        compiler_params=_params("parallel", "arbitrary"),
        name="ffn",
    )(x3, g, w_up, w_up, conv_w, conv_w, conv_b, conv_b, w_down, final_g)


def _pad_lanes(v, width):
    return jnp.pad(v.astype(F32), (0, width - v.shape[0]))[None, :]


def kernel(x, mem, positions, norm_mix_g, w_in, ret_norm_g, gdn_conv_w, gdn_a_log, gdn_dt_bias,
           gdn_norm_g, w_br_ret, w_br_gdn, w_br_dil, w_out, norm_xattn_g, norm_mem_g, xattn_wq,
           xattn_wkv, xattn_wo, norm_ffn_g, ffn_w_up, ffn_conv_w, ffn_conv_b, ffn_w_down,
           final_norm_g):
    b, t, d = x.shape
    depth = w_in.shape[0]
    mem_tokens = mem.shape[1]
    ab0 = 2 * RET_QK_W + 2 * RET_V_W + 3 * GDN_W
    ab1 = ab0 + 2 * GDN_HEADS

    pos3 = positions.astype(F32)[:, :, None]
    ret_inv = 1.0 / (RET_ROT_BASE ** jnp.linspace(0.0, 1.0, RET_QK_DIM // 2, dtype=F32))
    ret_inv_row = jnp.tile(ret_inv, RET_QK_W // ret_inv.shape[0])[None, :]
    dil_inv = ROPE_THETA ** (-jnp.arange(0, DIL_ROT_DIM, 2, dtype=F32) / DIL_ROT_DIM)
    dil_head = jnp.concatenate([dil_inv, dil_inv, jnp.zeros((DIL_HEAD_DIM - DIL_ROT_DIM,), F32)])
    dil_inv_row = jnp.tile(dil_head, DIL_HEADS)[None, :]

    mem2 = mem.reshape(b * mem_tokens, d)
    for l in range(depth):
        wl = w_in[l]
        gates_w = wl[:, wl.shape[1] - N_BRANCH * d:]
        w_main = jnp.concatenate([gates_w, wl[:, :ab0], wl[:, ab1:wl.shape[1] - N_BRANCH * d]],
                                 axis=1).astype(BF16)
        w_ab = jnp.pad(wl[:, ab0:ab1], ((0, 0), (0, GAB_PAD - (ab1 - ab0)))).astype(BF16)
        w_abt = wl[:, ab0:ab1].T.astype(BF16)
        alog_l = _pad_lanes(gdn_a_log[l], GAB_PAD)
        dt_l = _pad_lanes(gdn_dt_bias[l], GAB_PAD)
        alog_s = _pad_lanes(gdn_a_log[l], SUBLANES).reshape(SUBLANES, 1)
        dt_s = _pad_lanes(gdn_dt_bias[l], SUBLANES).reshape(SUBLANES, 1)

        x2 = x.reshape(b * t, d)
        proj, gab, gabt = _in_proj(x2, norm_mix_g[l][None, :], w_main, w_ab, w_abt, tm=1024, tn=1280)
        proj3 = proj.reshape(b, t, PROJ_W)
        o_ret = _retention(proj3, pos3, ret_inv_row, ret_norm_g[l])
        gabt_chunks = gabt.reshape(SUBLANES, b * t // GDN_CHUNK, GDN_CHUNK).transpose(1, 0, 2)
        o_gdn = _gdn(proj3, gab, gabt_chunks, gdn_conv_w[l], alog_l, dt_l, alog_s, dt_s,
                     gdn_norm_g[l][None, :])
        o_dil = _dilated(proj3, pos3, dil_inv_row)
        x2 = _merge(x2, proj, o_ret.reshape(b * t, RET_V_W), o_gdn.reshape(b * t, GDN_W),
                    o_dil.reshape(b * t, DIL_GW), w_br_ret[l].astype(BF16),
                    w_br_gdn[l].astype(BF16), w_br_dil[l].astype(BF16), w_out[l].astype(BF16),
                    tm=512)

        kv = _norm_matmul(mem2, norm_mem_g[l][None, :], xattn_wkv[l].astype(BF16),
                          tm=mem_tokens)
        x3 = _xattn(x2.reshape(b, t, d), norm_xattn_g[l][None, :],
                    kv.reshape(b, mem_tokens, 2 * XATTN_W), xattn_wq[l].astype(BF16),
                    xattn_wo[l].astype(BF16), tm=512)

        x = _ffn(x3, norm_ffn_g[l][None, :], ffn_w_up[l].astype(BF16), ffn_conv_w[l],
                 ffn_conv_b[l][None, :], ffn_w_down[l].astype(BF16), final_norm_g[None, :],
                 tf=256, final_norm=(l == depth - 1))
    return x
```

```python
import functools
import math

import jax
import jax.numpy as jnp
import numpy as np
from jax import lax
from jax.experimental import pallas as pl
from jax.experimental.pallas import tpu as pltpu

F32 = jnp.float32
BF16 = jnp.bfloat16

EPS = 1e-6
NEG_INF = -1e30

RET_HEADS, RET_QK_DIM, RET_V_DIM, RET_CHUNK = 4, 64, 128, 128
RET_ROT_BASE = 10000.0
GDN_HEADS, GDN_HEAD_DIM, GDN_CONV, GDN_CHUNK = 4, 128, 4, 64
DIL_GROUPS = ((128, 1), (512, 4), (2048, 16))
DIL_HEADS, DIL_HEAD_DIM = 4, 64
DIL_ROT_DIM = DIL_HEAD_DIM // 4
DIL_SPAN = 128
ROPE_THETA = 500000.0
XATTN_HEADS, XATTN_HEAD_DIM = 4, 128
FFN_CONV = 3
N_BRANCH = 3

RET_QK_W = RET_HEADS * RET_QK_DIM
RET_V_W = RET_HEADS * RET_V_DIM
GDN_W = GDN_HEADS * GDN_HEAD_DIM
DIL_GW = DIL_HEADS * DIL_HEAD_DIM
DIL_W = len(DIL_GROUPS) * DIL_GW
XATTN_W = XATTN_HEADS * XATTN_HEAD_DIM

LANES = 128
SUBLANES = 8
VMEM_LIMIT_BYTES = 56 * 1024 * 1024

COL_GATES = 0
COL_RQ = COL_GATES + N_BRANCH * 1024
COL_RK = COL_RQ + RET_QK_W
COL_RV = COL_RK + RET_QK_W
COL_RG = COL_RV + RET_V_W
COL_GQKV = COL_RG + RET_V_W
COL_GZ = COL_GQKV + 3 * GDN_W
COL_DQ = COL_GZ + GDN_W
COL_DK = COL_DQ + DIL_W
COL_DV = COL_DK + DIL_W
PROJ_W = COL_DV + DIL_W
GAB_PAD = LANES


def _params(*sem):
    return pltpu.CompilerParams(dimension_semantics=sem, vmem_limit_bytes=VMEM_LIMIT_BYTES)


def _rms(x, g):
    return x * lax.rsqrt(jnp.mean(x * x, axis=-1, keepdims=True) + EPS) * g


def _dot(a, b):
    return jnp.dot(a, b, preferred_element_type=F32)


def _dot_nt(a, b):
    return lax.dot_general(a, b, (((1,), (1,)), ((), ())), preferred_element_type=F32)


def _dot_tn(a, b):
    return lax.dot_general(a, b, (((0,), (0,)), ((), ())), preferred_element_type=F32)


def _silu(x):
    return x * jax.nn.sigmoid(x)


def _lane_sum_bcast(x):
    hi = x.astype(BF16)
    lo = (x - hi.astype(F32)).astype(BF16)
    ones = jnp.ones((2 * LANES, LANES), BF16)
    return _dot(jnp.concatenate([hi, lo], axis=1), ones)


def _iota2(shape, dim):
    return lax.broadcasted_iota(jnp.int32, shape, dim)


def _in_proj_kernel(x_ref, g_ref, w_ref, wab_ref, wabt_ref, proj_ref, gab_ref, gabt_ref, h_scr):
    @pl.when(pl.program_id(1) == 0)
    def _():
        hb = _rms(x_ref[...], g_ref[...]).astype(BF16)
        h_scr[...] = hb
        gab_ref[...] = _dot(hb, wab_ref[...])
        gabt_ref[...] = _dot_nt(wabt_ref[...], hb)

    proj_ref[...] = _dot(h_scr[...], w_ref[...]).astype(BF16)


def _in_proj(x2, g, w_main, w_ab, w_abt, *, tm, tn):
    m, d = x2.shape
    n = w_main.shape[1]
    return pl.pallas_call(
        _in_proj_kernel,
        grid=(m // tm, n // tn),
        in_specs=[
            pl.BlockSpec((tm, d), lambda i, j: (i, 0)),
            pl.BlockSpec((1, d), lambda i, j: (0, 0)),
            pl.BlockSpec((d, tn), lambda i, j: (0, j)),
            pl.BlockSpec((d, GAB_PAD), lambda i, j: (0, 0)),
            pl.BlockSpec((SUBLANES, d), lambda i, j: (0, 0)),
        ],
        out_specs=[
            pl.BlockSpec((tm, tn), lambda i, j: (i, j)),
            pl.BlockSpec((tm, GAB_PAD), lambda i, j: (i, 0)),
            pl.BlockSpec((SUBLANES, tm), lambda i, j: (0, i)),
        ],
        out_shape=[
            jax.ShapeDtypeStruct((m, n), BF16),
            jax.ShapeDtypeStruct((m, GAB_PAD), F32),
            jax.ShapeDtypeStruct((SUBLANES, m), F32),
        ],
        scratch_shapes=[pltpu.VMEM((tm, d), BF16)],
        compiler_params=_params("parallel", "arbitrary"),
        name="in_proj",
    )(x2, g, w_main, w_ab, w_abt)


def _norm_matmul_kernel(x_ref, g_ref, w_ref, o_ref):
    hb = _rms(x_ref[...], g_ref[...]).astype(BF16)
    o_ref[...] = _dot(hb, w_ref[...]).astype(BF16)


def _norm_matmul(x2, g, w, *, tm):
    m, d = x2.shape
    n = w.shape[1]
    return pl.pallas_call(
        _norm_matmul_kernel,
        grid=(m // tm,),
        in_specs=[
            pl.BlockSpec((tm, d), lambda i: (i, 0)),
            pl.BlockSpec((1, d), lambda i: (0, 0)),
            pl.BlockSpec((d, n), lambda i: (0, 0)),
        ],
        out_specs=pl.BlockSpec((tm, n), lambda i: (i, 0)),
        out_shape=jax.ShapeDtypeStruct((m, n), BF16),
        compiler_params=_params("parallel"),
        name="norm_matmul",
    )(x2, g, w)


def _ret_kernel(pos_ref, inv_ref, q_ref, k_ref, v_ref, gate_ref, ng_ref, o_ref, state):
    t = q_ref.shape[1]
    c = RET_CHUNK
    nh = RET_HEADS
    state[...] = jnp.zeros_like(state)

    lane = _iota2((1, RET_QK_W), 1)
    first_half = (lane % RET_QK_DIM) < (RET_QK_DIM // 2)
    head_of_lane = lane // RET_QK_DIM
    diff = (_iota2((c, c), 0) - _iota2((c, c), 1)).astype(F32)
    idx_col = _iota2((c, 1), 0).astype(F32)
    idx_row = _iota2((1, c), 1).astype(F32)
    log_gamma = [math.log1p(-(2.0 ** (-5.0 - h))) for h in range(nh)]
    decay_all = jnp.concatenate(
        [jnp.where(diff >= 0, jnp.exp(jnp.maximum(diff, 0.0) * lg), 0.0) for lg in log_gamma], axis=0)
    q_decay_all = jnp.concatenate([jnp.exp((idx_col + 1.0) * lg) for lg in log_gamma], axis=0)
    k_decay = [jnp.exp((c - 1.0 - idx_row) * lg) for lg in log_gamma]
    chunk_decay = [math.exp(c * lg) for lg in log_gamma]
    ng_all = jnp.concatenate(
        [jnp.broadcast_to(ng_ref[h:h + 1, :], (c, RET_V_DIM)) for h in range(nh)], axis=0)

    def rope(z, cs, sn):
        partner = jnp.where(first_half, -pltpu.roll(z, RET_QK_W - RET_QK_DIM // 2, 1),
                            pltpu.roll(z, RET_QK_DIM // 2, 1))
        return z * cs + partner * sn

    def chunk(ci, carry):
        r0 = pl.multiple_of(ci * c, c)
        rows = pl.ds(r0, c)
        ang = pos_ref[0, rows, :] * inv_ref[...]
        cs, sn = jnp.cos(ang), jnp.sin(ang)
        cs = jnp.concatenate([cs, cs], axis=1)
        sn = jnp.concatenate([sn, sn], axis=1)
        q = rope(q_ref[0, rows, :].astype(F32), cs, sn)
        k = rope(k_ref[0, rows, :].astype(F32), cs, sn) * (RET_QK_DIM ** -0.5)
        k_t = k.T
        k_t_bf = k_t.astype(BF16)
        q_st = jnp.concatenate([jnp.where(head_of_lane == h, q, 0.0) for h in range(nh)], axis=0)
        s_bf = (_dot(q_st.astype(BF16), k_t_bf) * decay_all).astype(BF16)
        qd_bf = (q_st * q_decay_all).astype(BF16)
        vs = [v_ref[0, rows, h * RET_V_DIM:(h + 1) * RET_V_DIM] for h in range(nh)]
        sts = [state[h] for h in range(nh)]
        intra = [_dot(s_bf[h * c:(h + 1) * c], vs[h]) for h in range(nh)]
        inter = [_dot(qd_bf[h * c:(h + 1) * c], sts[h].astype(BF16)) for h in range(nh)]
        upd = [_dot((k_t * k_decay[h]).astype(BF16), vs[h]) for h in range(nh)]
        for h in range(nh):
            state[h] = sts[h] * chunk_decay[h] + upd[h]
        o_all = jnp.concatenate([intra[h] + inter[h] for h in range(nh)], axis=0)
        ss = _lane_sum_bcast(o_all * o_all)
        y_all = o_all * lax.rsqrt(ss * (1.0 / RET_V_DIM) + EPS) * ng_all
        for h in range(nh):
            cols = slice(h * RET_V_DIM, (h + 1) * RET_V_DIM)
            y = y_all[h * c:(h + 1) * c] * _silu(gate_ref[0, rows, cols].astype(F32))
            o_ref[0, rows, cols] = y.astype(BF16)
        return carry

    lax.fori_loop(0, t // c, chunk, 0)


def _retention(proj3, pos3, inv_row, norm_g):
    b, t, _ = proj3.shape
    return pl.pallas_call(
        _ret_kernel,
        grid=(b,),
        in_specs=[
            pl.BlockSpec((1, t, 1), lambda i: (i, 0, 0)),
            pl.BlockSpec((1, LANES), lambda i: (0, 0)),
            pl.BlockSpec((1, t, RET_QK_W), lambda i: (i, 0, COL_RQ // RET_QK_W)),
            pl.BlockSpec((1, t, RET_QK_W), lambda i: (i, 0, COL_RK // RET_QK_W)),
            pl.BlockSpec((1, t, RET_V_W), lambda i: (i, 0, COL_RV // RET_V_W)),
            pl.BlockSpec((1, t, RET_V_W), lambda i: (i, 0, COL_RG // RET_V_W)),
            pl.BlockSpec((RET_HEADS, RET_V_DIM), lambda i: (0, 0)),
        ],
        out_specs=pl.BlockSpec((1, t, RET_V_W), lambda i: (i, 0, 0)),
        out_shape=jax.ShapeDtypeStruct((b, t, RET_V_W), BF16),
        scratch_shapes=[pltpu.VMEM((RET_HEADS, RET_QK_W, RET_V_DIM), F32)],
        compiler_params=_params("parallel"),
        name="retention",
    )(pos3, inv_row, proj3, proj3, proj3, proj3, norm_g)


GDN_CHUNKS_PER_STEP = 4


def _softplus(x):
    return jnp.maximum(x, 0.0) + jnp.log1p(jnp.exp(-jnp.abs(x)))


def _gdn_kernel(qkv_ref, z_ref, gab_ref, gabt_ref, cw_ref, alog_l_ref, dt_l_ref, alog_r_ref,
                dt_r_ref, ng_ref, o_ref, state):
    t = qkv_ref.shape[1]
    c = GDN_CHUNK
    hd = GDN_HEAD_DIM
    nh = GDN_HEADS
    m = nh * c
    win = 2 * c
    state[...] = jnp.zeros_like(state)

    row, col = _iota2((m, m), 0), _iota2((m, m), 1)
    same_head = (row // c) == (col // c)
    tri_bd = same_head & (row >= col)
    strict_bd = same_head & (row > col)
    triu_bd = (same_head & (row <= col)).astype(F32)
    tril_c = (_iota2((c, c), 0) >= _iota2((c, c), 1)).astype(F32)
    n_shifted = GDN_CONV - 1
    sel_row, sel_col = _iota2((n_shifted * c, win), 0), _iota2((n_shifted * c, win), 1)
    sel_target = (sel_row % c) + (sel_row // c) - n_shifted
    scale = hd ** -0.5
    n_doublings = int(math.log2(c)) - 1
    ng_all = jnp.broadcast_to(ng_ref[...], (m, hd))

    def stack_heads(y, base):
        return jnp.concatenate([y[:, base + h * hd:base + (h + 1) * hd] for h in range(nh)], axis=0)

    def stack_cols(a, first):
        return jnp.concatenate([a[:, first + h:first + h + 1] for h in range(nh)], axis=0)

    def prepare(ci):
        r0 = pl.multiple_of(ci * c, c)
        rows = pl.ds(r0, c)
        start = pl.multiple_of(jnp.maximum(r0 - c, 0), c)
        shift = jnp.where(ci > 0, c, 0)
        sel = (sel_col == sel_target + shift).astype(BF16)
        shifted = _dot(sel, qkv_ref[0, pl.ds(start, win), :])
        y = cw_ref[n_shifted:GDN_CONV, :] * qkv_ref[0, rows, :].astype(F32)
        for j in range(n_shifted):
            y = y + cw_ref[j:j + 1, :] * shifted[j * c:(j + 1) * c]
        y = _silu(y)
        q, k, v = stack_heads(y, 0), stack_heads(y, GDN_W), stack_heads(y, 2 * GDN_W)

        gab = gab_ref[rows, :]
        g_col = -jnp.exp(alog_l_ref[...]) * _softplus(gab + dt_l_ref[...])
        gc_col = jnp.dot(tril_c, g_col, precision=lax.Precision.HIGHEST,
                         preferred_element_type=F32)
        beta = stack_cols(jax.nn.sigmoid(gab), nh)
        gcc = stack_cols(gc_col, 0)
        g_last = [gc_col[c - 1:c, h:h + 1] for h in range(nh)]
        g_last_all = jnp.concatenate([jnp.broadcast_to(g, (c, 1)) for g in g_last], axis=0)
        g_row = -jnp.exp(alog_r_ref[...]) * _softplus(gabt_ref[ci] + dt_r_ref[...])
        gcr = jnp.dot(jnp.broadcast_to(g_row, (SUBLANES, m)), triu_bd,
                      precision=lax.Precision.HIGHEST, preferred_element_type=F32)[0:1, :]
        decay = jnp.where(tri_bd, jnp.exp(jnp.where(tri_bd, gcc - gcr, 0.0)), 0.0)

        q = q * lax.rsqrt(_lane_sum_bcast(q * q) + EPS) * scale
        k = k * lax.rsqrt(_lane_sum_bcast(k * k) + EPS)
        k_t_bf = k.T.astype(BF16)
        kb = k * beta
        eg = jnp.exp(gcc)
        raw = _dot(jnp.concatenate([kb, q], axis=0).astype(BF16), k_t_bf)
        n_mat = jnp.where(strict_bd, raw[0:m] * decay, 0.0)
        attn = jnp.where(tri_bd, raw[m:2 * m] * decay, 0.0)
        x = jnp.concatenate([v * beta, kb * eg], axis=1)
        return dict(rows=rows, n_pow=n_mat, x=x, attn_bf=attn.astype(BF16),
                    qg_bf=(q * eg).astype(BF16), kg_bf=(k * jnp.exp(g_last_all - gcc)).astype(BF16),
                    g_last=g_last)

    def step(si, carry):
        chunks = [prepare(si * GDN_CHUNKS_PER_STEP + j) for j in range(GDN_CHUNKS_PER_STEP)]
        for d in chunks:
            d["x"] = d["x"] - _dot(d["n_pow"].astype(BF16), d["x"].astype(BF16))
        for _ in range(n_doublings):
            for d in chunks:
                p_bf = d["n_pow"].astype(BF16)
                d["n_pow"] = _dot(p_bf, p_bf)
            for d in chunks:
                d["x"] = d["x"] + _dot(d["n_pow"].astype(BF16), d["x"].astype(BF16))
        for d in chunks:
            u, w_bf = d["x"][:, :hd], d["x"][:, hd:].astype(BF16)
            sts = [state[h] for h in range(nh)]
            sts_bf = [s.astype(BF16) for s in sts]
            hrows = [slice(h * c, (h + 1) * c) for h in range(nh)]
            v_new = [u[hrows[h]] - _dot(w_bf[hrows[h]], sts_bf[h]) for h in range(nh)]
            o_state = [_dot(d["qg_bf"][hrows[h]], sts_bf[h]) for h in range(nh)]
            v_new_bf = [vn.astype(BF16) for vn in v_new]
            for h in range(nh):
                state[h] = sts[h] * jnp.exp(d["g_last"][h]) + _dot_tn(d["kg_bf"][hrows[h]], v_new_bf[h])
            o = jnp.concatenate(o_state, axis=0) + _dot(d["attn_bf"], jnp.concatenate(v_new_bf, axis=0))
            y = o * lax.rsqrt(_lane_sum_bcast(o * o) * (1.0 / hd) + EPS) * ng_all
            for h in range(nh):
                cols = slice(h * hd, (h + 1) * hd)
                out = y[hrows[h]] * _silu(z_ref[0, d["rows"], cols].astype(F32))
                o_ref[0, d["rows"], cols] = out.astype(BF16)
        return carry

    lax.fori_loop(0, t // c // GDN_CHUNKS_PER_STEP, step, 0)


def _gdn(proj3, gab, gabt_rows, conv_w, alog_l, dt_l, alog_r, dt_r, norm_g):
    b, t, _ = proj3.shape
    m = GDN_HEADS * GDN_CHUNK
    return pl.pallas_call(
        _gdn_kernel,
        grid=(b,),
        in_specs=[
            pl.BlockSpec((1, t, 3 * GDN_W), lambda i: (i, 0, COL_GQKV // (3 * GDN_W))),
            pl.BlockSpec((1, t, GDN_W), lambda i: (i, 0, COL_GZ // GDN_W)),
            pl.BlockSpec((t, GAB_PAD), lambda i: (i, 0)),
            pl.BlockSpec((t // GDN_CHUNK, 1, m), lambda i: (i, 0, 0)),
            pl.BlockSpec((GDN_CONV, 3 * GDN_W), lambda i: (0, 0)),
            pl.BlockSpec((1, GAB_PAD), lambda i: (0, 0)),
            pl.BlockSpec((1, GAB_PAD), lambda i: (0, 0)),
            pl.BlockSpec((1, m), lambda i: (0, 0)),
            pl.BlockSpec((1, m), lambda i: (0, 0)),
            pl.BlockSpec((1, GDN_HEAD_DIM), lambda i: (0, 0)),
        ],
        out_specs=pl.BlockSpec((1, t, GDN_W), lambda i: (i, 0, 0)),
        out_shape=jax.ShapeDtypeStruct((b, t, GDN_W), BF16),
        scratch_shapes=[pltpu.VMEM((GDN_HEADS, GDN_HEAD_DIM, GDN_HEAD_DIM), F32)],
        compiler_params=_params("parallel"),
        name="gdn",
    )(proj3, proj3, gab, gabt_rows, conv_w, alog_l, dt_l, alog_r, dt_r, norm_g)


DIL_BLOCKS_PER_STEP = 4


def _split_store(ref, lead, rows, val):
    for half in range(val.shape[1] // LANES):
        ref[lead + (half, rows, slice(None))] = val[:, half * LANES:(half + 1) * LANES]


def _split_load(ref, lead, rows):
    return jnp.concatenate([ref[lead + (half, rows, slice(None))] for half in range(DIL_GW // LANES)],
                           axis=1)


def _dil_group(g, dilation, qs, ks, vs, o_scr, lse_scr):
    t = qs.shape[1]
    span = DIL_SPAN
    nh = DIL_HEADS
    nb = t // dilation // span
    nk = 2 * span if nb > 1 else span
    head_of_lane = _iota2((1, DIL_GW), 1) // DIL_HEAD_DIM
    iq = _iota2((nh * span, nk), 0) % span
    ik = _iota2((nh * span, nk), 1)
    if nb > 1:
        dist = span + iq - ik
        band = (dist >= 0) & (dist <= span)
    else:
        band = iq >= ik

    def rows_of(r, n):
        if dilation == 1:
            return pl.ds(pl.multiple_of(span * n, span), span)
        return pl.ds(r + dilation * span * n, span, stride=dilation)

    def scores(r, n):
        cur = rows_of(r, n)
        qb = _split_load(qs, (), cur)
        if nb > 1:
            prev = rows_of(r, jnp.maximum(n - 1, 0) if dilation == 1 else max(n - 1, 0))
            kk = jnp.concatenate([_split_load(ks, (), prev), _split_load(ks, (), cur)], axis=0)
            vv = jnp.concatenate([_split_load(vs, (), prev), _split_load(vs, (), cur)], axis=0)
            valid = band & (ik >= jnp.where(n > 0, 0, span))
        else:
            kk, vv, valid = _split_load(ks, (), cur), _split_load(vs, (), cur), band
        q_st = jnp.concatenate([jnp.where(head_of_lane == h, qb, 0.0) for h in range(nh)], axis=0)
        s = _dot_nt(q_st.astype(BF16), kk.astype(BF16))
        s = jnp.where(valid, s, NEG_INF)
        m = jnp.max(s, axis=-1, keepdims=True)
        p = jnp.exp(s - m)
        den = jnp.sum(p, axis=-1, keepdims=True)
        pn = (p * (1.0 / den)).astype(BF16)
        return cur, pn, vv, m + jnp.log(den)

    def outputs(cur, pn, vv, lse):
        p_cat = jnp.concatenate([pn[h * span:(h + 1) * span] for h in range(nh)], axis=1)
        v_st = jnp.concatenate([jnp.where(head_of_lane == h, vv, 0.0) for h in range(nh)],
                               axis=0).astype(BF16)
        lse_x = jnp.zeros((span, DIL_GW), F32)
        for h in range(nh):
            lse_x = jnp.where(head_of_lane == h, lse[h * span:(h + 1) * span], lse_x)
        _split_store(o_scr, (g,), cur, _dot(p_cat, v_st))
        _split_store(lse_scr, (g,), cur, lse_x)

    def run(blocks):
        staged = [scores(r, n) for r, n in blocks]
        for args in staged:
            outputs(*args)

    step = min(DIL_BLOCKS_PER_STEP, nb * dilation)
    if dilation == 1:
        def body(i, carry):
            run([(0, i * step + j) for j in range(step)])
            return carry
        lax.fori_loop(0, nb // step, body, 0)
    else:
        blocks = [(r, n) for r in range(dilation) for n in range(nb)]
        for i in range(0, len(blocks), step):
            run(blocks[i:i + step])


def _dil_kernel(pos_ref, inv_ref, q_ref, k_ref, v_ref, o_ref, cs_scr, s1_scr, s2_scr, qs, ks, vs,
                o_scr, lse_scr):
    g = pl.program_id(1)
    half = DIL_ROT_DIM // 2

    @pl.when(g == 0)
    def _():
        lane = _iota2((1, LANES), 1) % DIL_HEAD_DIM
        ang = pos_ref[0] * inv_ref[...]
        sn = jnp.sin(ang)
        cs_scr[...] = jnp.cos(ang)
        s1_scr[...] = jnp.where(lane < half, -sn, 0.0)
        s2_scr[...] = jnp.where((lane >= half) & (lane < DIL_ROT_DIM), sn, 0.0)

    def both(ref):
        a = ref[...]
        return jnp.concatenate([a, a], axis=1)

    def rope(z):
        return (z * both(cs_scr) + pltpu.roll(z, DIL_GW - half, 1) * both(s1_scr)
                + pltpu.roll(z, half, 1) * both(s2_scr))

    everything = slice(None)
    _split_store(qs, (), everything, rope(q_ref[0].astype(F32)) * (DIL_HEAD_DIM ** -0.5))
    _split_store(ks, (), everything, rope(k_ref[0].astype(F32)))
    _split_store(vs, (), everything, v_ref[0].astype(F32))

    for gi, (window, dilation) in enumerate(DIL_GROUPS):
        assert window // dilation == DIL_SPAN

        @pl.when(g == gi)
        def _(gi=gi, dilation=dilation):
            _dil_group(gi, dilation, qs, ks, vs, o_scr, lse_scr)

    @pl.when(g == len(DIL_GROUPS) - 1)
    def _():
        for half_i in range(DIL_GW // LANES):
            lses = [lse_scr[i, half_i] for i in range(len(DIL_GROUPS))]
            m = functools.reduce(jnp.maximum, lses)
            es = [jnp.exp(l - m) for l in lses]
            den = functools.reduce(lambda a, b: a + b, es)
            num = functools.reduce(lambda a, b: a + b,
                                   [e * o_scr[i, half_i] for i, e in enumerate(es)])
            o_ref[0, :, half_i * LANES:(half_i + 1) * LANES] = (num / den).astype(BF16)


def _dilated(proj3, pos3, inv_row):
    b, t, _ = proj3.shape
    ng = len(DIL_GROUPS)
    table = pltpu.VMEM((t, LANES), F32)
    split = pltpu.VMEM((DIL_GW // LANES, t, LANES), F32)
    return pl.pallas_call(
        _dil_kernel,
        grid=(b, ng),
        in_specs=[
            pl.BlockSpec((1, t, 1), lambda i, g: (i, 0, 0)),
            pl.BlockSpec((1, LANES), lambda i, g: (0, 0)),
            pl.BlockSpec((1, t, DIL_GW), lambda i, g: (i, 0, COL_DQ // DIL_GW + g)),
            pl.BlockSpec((1, t, DIL_GW), lambda i, g: (i, 0, COL_DK // DIL_GW + g)),
            pl.BlockSpec((1, t, DIL_GW), lambda i, g: (i, 0, COL_DV // DIL_GW + g)),
        ],
        out_specs=pl.BlockSpec((1, t, DIL_GW), lambda i, g: (i, 0, 0)),
        out_shape=jax.ShapeDtypeStruct((b, t, DIL_GW), BF16),
        scratch_shapes=[table, table, table, split, split, split,
                        pltpu.VMEM((ng, DIL_GW // LANES, t, LANES), F32),
                        pltpu.VMEM((ng, DIL_GW // LANES, t, LANES), F32)],
        compiler_params=_params("parallel", "arbitrary"),
        name="dilated",
    )(pos3, inv_row, proj3, proj3, proj3)


def _merge_kernel(x_ref, gates_ref, oret_ref, ogdn_ref, odil_ref, wr_ref, wg_ref, wd_ref, wo_ref,
                  out_ref):
    d = x_ref.shape[1]
    branches = (_dot(oret_ref[...], wr_ref[...]), _dot(ogdn_ref[...], wg_ref[...]),
                _dot(odil_ref[...], wd_ref[...]))
    merged = None
    for bi, br in enumerate(branches):
        term = jax.nn.sigmoid(gates_ref[:, bi * d:(bi + 1) * d].astype(F32)) * br
        merged = term if merged is None else merged + term
    out_ref[...] = x_ref[...] + _dot(merged.astype(BF16), wo_ref[...])


def _merge(x2, proj2, o_ret, o_gdn, o_dil, w_ret, w_gdn, w_dil, w_out, *, tm):
    m, d = x2.shape
    full = lambda a: pl.BlockSpec(a.shape, lambda i: (0, 0))
    return pl.pallas_call(
        _merge_kernel,
        grid=(m // tm,),
        in_specs=[
            pl.BlockSpec((tm, d), lambda i: (i, 0)),
            pl.BlockSpec((tm, N_BRANCH * d), lambda i: (i, COL_GATES // (N_BRANCH * d))),
            pl.BlockSpec((tm, RET_V_W), lambda i: (i, 0)),
            pl.BlockSpec((tm, GDN_W), lambda i: (i, 0)),
            pl.BlockSpec((tm, DIL_GW), lambda i: (i, 0)),
            full(w_ret), full(w_gdn), full(w_dil), full(w_out),
        ],
        out_specs=pl.BlockSpec((tm, d), lambda i: (i, 0)),
        out_shape=jax.ShapeDtypeStruct((m, d), F32),
        compiler_params=_params("parallel"),
        name="merge",
    )(x2, proj2, o_ret, o_gdn, o_dil, w_ret, w_gdn, w_dil, w_out)


def _xattn_kernel(x_ref, g_ref, kv_ref, wq_ref, wo_ref, out_ref):
    x = x_ref[0]
    hb = _rms(x, g_ref[...]).astype(BF16)
    q = _dot(hb, wq_ref[...])
    outs = []
    for h in range(XATTN_HEADS):
        cols = slice(h * XATTN_HEAD_DIM, (h + 1) * XATTN_HEAD_DIM)
        kh = kv_ref[0, :, cols]
        vh = kv_ref[0, :, XATTN_W + h * XATTN_HEAD_DIM:XATTN_W + (h + 1) * XATTN_HEAD_DIM]
        s = _dot_nt(q[:, cols].astype(BF16), kh) * (XATTN_HEAD_DIM ** -0.5)
        e = jnp.exp(s - jnp.max(s, axis=-1, keepdims=True))
        p = e / jnp.sum(e, axis=-1, keepdims=True)
        outs.append(_dot(p.astype(BF16), vh))
    o = jnp.concatenate(outs, axis=1).astype(BF16)
    out_ref[0] = x + _dot(o, wo_ref[...])


def _xattn(x3, g, kv3, wq, wo, *, tm):
    b, t, d = x3.shape
    mem = kv3.shape[1]
    return pl.pallas_call(
        _xattn_kernel,
        grid=(b, t // tm),
        in_specs=[
            pl.BlockSpec((1, tm, d), lambda i, j: (i, j, 0)),
            pl.BlockSpec((1, d), lambda i, j: (0, 0)),
            pl.BlockSpec((1, mem, 2 * XATTN_W), lambda i, j: (i, 0, 0)),
            pl.BlockSpec(wq.shape, lambda i, j: (0, 0)),
            pl.BlockSpec(wo.shape, lambda i, j: (0, 0)),
        ],
        out_specs=pl.BlockSpec((1, tm, d), lambda i, j: (i, j, 0)),
        out_shape=jax.ShapeDtypeStruct((b, t, d), F32),
        compiler_params=_params("parallel", "parallel"),
        name="xattn",
    )(x3, g, kv3, wq, wo)


def _ffn_kernel(x_ref, g_ref, wup_ref, cw_ref, cb_ref, wd_ref, fg_ref, out_ref,
                h_scr, act_scr, a_scr, u_scr, hist_a, hist_u, *, tf, final_norm):
    tm = x_ref.shape[1]
    ffn_dim = wd_ref.shape[0]
    halo = SUBLANES

    @pl.when(pl.program_id(1) == 0)
    def _():
        hist_a[...] = jnp.zeros_like(hist_a)
        hist_u[...] = jnp.zeros_like(hist_u)

    x = x_ref[0]
    h_scr[...] = _rms(x, g_ref[...]).astype(BF16)

    def up_conv(scr, hist, c, col0):
        cols = slice(col0, col0 + tf)
        up = _dot(h_scr[...], wup_ref[:, cols])
        scr[0:halo, :] = hist[c]
        scr[halo:, :] = up
        hist[c] = up[tm - halo:tm]
        acc = cb_ref[:, cols]
        for j in range(FFN_CONV):
            off = halo - (FFN_CONV - 1) + j
            acc = acc + cw_ref[j:j + 1, cols] * scr[off:off + tm, :]
        return acc

    for c in range(ffn_dim // tf):
        a = up_conv(a_scr, hist_a, c, c * tf)
        u = up_conv(u_scr, hist_u, c, ffn_dim + c * tf)
        act_scr[:, c * tf:(c + 1) * tf] = (_silu(a) * u).astype(BF16)

    out = x + _dot(act_scr[...], wd_ref[...])
    if final_norm:
        out = _rms(out, fg_ref[...])
    out_ref[0] = out


def _ffn(x3, g, w_up, conv_w, conv_b, w_down, final_g, *, tm, tf, final_norm):
    b, t, d = x3.shape
    ffn_dim = w_down.shape[0]
    nf = ffn_dim // tf
    resident = lambda a: pl.BlockSpec(a.shape, lambda i, j: (0, 0), pipeline_mode=pl.Buffered(1))
    return pl.pallas_call(
        functools.partial(_ffn_kernel, tf=tf, final_norm=final_norm),
        grid=(b, t // tm),
        in_specs=[
            pl.BlockSpec((1, tm, d), lambda i, j: (i, j, 0)),
            resident(g), resident(w_up), resident(conv_w), resident(conv_b), resident(w_down),
            resident(final_g),
        ],
        out_specs=pl.BlockSpec((1, tm, d), lambda i, j: (i, j, 0)),
        out_shape=jax.ShapeDtypeStruct((b, t, d), F32),
        scratch_shapes=[
            pltpu.VMEM((tm, d), BF16),
            pltpu.VMEM((tm, ffn_dim), BF16),
            pltpu.VMEM((tm + SUBLANES, tf), F32),
            pltpu.VMEM((tm + SUBLANES, tf), F32),
            pltpu.VMEM((nf, SUBLANES, tf), F32),
            pltpu.VMEM((nf, SUBLANES, tf), F32),
        ],
        compiler_params=_params("parallel", "arbitrary"),
        name="ffn",
    )(x3, g, w_up, conv_w, conv_b, w_down, final_g)


def _pad_lanes(v, width):
    return jnp.pad(v.astype(F32), (0, width - v.shape[0]))[None, :]


def kernel(x, mem, positions, norm_mix_g, w_in, ret_norm_g, gdn_conv_w, gdn_a_log, gdn_dt_bias,
           gdn_norm_g, w_br_ret, w_br_gdn, w_br_dil, w_out, norm_xattn_g, norm_mem_g, xattn_wq,
           xattn_wkv, xattn_wo, norm_ffn_g, ffn_w_up, ffn_conv_w, ffn_conv_b, ffn_w_down,
           final_norm_g):
    b, t, d = x.shape
    depth = w_in.shape[0]
    mem_tokens = mem.shape[1]
    ab0 = 2 * RET_QK_W + 2 * RET_V_W + 3 * GDN_W
    ab1 = ab0 + 2 * GDN_HEADS

    pos3 = positions.astype(F32)[:, :, None]
    ret_inv = 1.0 / (RET_ROT_BASE ** jnp.linspace(0.0, 1.0, RET_QK_DIM // 2, dtype=F32))
    ret_inv_row = jnp.tile(ret_inv, LANES // ret_inv.shape[0])[None, :]
    dil_inv = ROPE_THETA ** (-jnp.arange(0, DIL_ROT_DIM, 2, dtype=F32) / DIL_ROT_DIM)
    dil_head = jnp.concatenate([dil_inv, dil_inv, jnp.zeros((DIL_HEAD_DIM - DIL_ROT_DIM,), F32)])
    dil_inv_row = jnp.tile(dil_head, LANES // DIL_HEAD_DIM)[None, :]

    mem2 = mem.reshape(b * mem_tokens, d)
    for l in range(depth):
        wl = w_in[l]
        gates_w = wl[:, wl.shape[1] - N_BRANCH * d:]
        w_main = jnp.concatenate([gates_w, wl[:, :ab0], wl[:, ab1:wl.shape[1] - N_BRANCH * d]],
                                 axis=1).astype(BF16)
        w_ab = jnp.pad(wl[:, ab0:ab1], ((0, 0), (0, GAB_PAD - (ab1 - ab0)))).astype(BF16)
        w_abt = wl[:, ab0:ab1].T.astype(BF16)
        alog_l = _pad_lanes(gdn_a_log[l], GAB_PAD)
        dt_l = _pad_lanes(gdn_dt_bias[l], GAB_PAD)
        alog_r = jnp.repeat(gdn_a_log[l].astype(F32), GDN_CHUNK)[None, :]
        dt_r = jnp.repeat(gdn_dt_bias[l].astype(F32), GDN_CHUNK)[None, :]

        x2 = x.reshape(b * t, d)
        proj, gab, gabt = _in_proj(x2, norm_mix_g[l][None, :], w_main, w_ab, w_abt, tm=1024, tn=1280)
        proj3 = proj.reshape(b, t, PROJ_W)
        o_ret = _retention(proj3, pos3, ret_inv_row, ret_norm_g[l])
        n_chunks = b * t // GDN_CHUNK
        gabt_rows = (gabt[:GDN_HEADS].reshape(GDN_HEADS, n_chunks, GDN_CHUNK).transpose(1, 0, 2)
                     .reshape(n_chunks, 1, GDN_HEADS * GDN_CHUNK))
        o_gdn = _gdn(proj3, gab, gabt_rows, gdn_conv_w[l], alog_l, dt_l, alog_r, dt_r,
                     gdn_norm_g[l][None, :])
        o_dil = _dilated(proj3, pos3, dil_inv_row)
        x2 = _merge(x2, proj, o_ret.reshape(b * t, RET_V_W), o_gdn.reshape(b * t, GDN_W),
                    o_dil.reshape(b * t, DIL_GW), w_br_ret[l].astype(BF16),
                    w_br_gdn[l].astype(BF16), w_br_dil[l].astype(BF16), w_out[l].astype(BF16),
                    tm=512)

        kv = _norm_matmul(mem2, norm_mem_g[l][None, :], xattn_wkv[l].astype(BF16),
                          tm=mem_tokens)
        x3 = _xattn(x2.reshape(b, t, d), norm_xattn_g[l][None, :],
                    kv.reshape(b, mem_tokens, 2 * XATTN_W), xattn_wq[l].astype(BF16),
                    xattn_wo[l].astype(BF16), tm=512)

        x = _ffn(x3, norm_ffn_g[l][None, :], ffn_w_up[l].astype(BF16), ffn_conv_w[l],
                 ffn_conv_b[l][None, :], ffn_w_down[l].astype(BF16), final_norm_g[None, :],
                 tm=512, tf=256, final_norm=(l == depth - 1))
    return x
```

```python
import functools
import math

import jax
import jax.numpy as jnp
import numpy as np
from jax import lax
from jax.experimental import pallas as pl
from jax.experimental.pallas import tpu as pltpu

F32 = jnp.float32
BF16 = jnp.bfloat16

EPS = 1e-6
NEG_INF = -1e30

RET_HEADS, RET_QK_DIM, RET_V_DIM, RET_CHUNK = 4, 64, 128, 128
RET_ROT_BASE = 10000.0
GDN_HEADS, GDN_HEAD_DIM, GDN_CONV, GDN_CHUNK = 4, 128, 4, 64
DIL_GROUPS = ((128, 1), (512, 4), (2048, 16))
DIL_HEADS, DIL_HEAD_DIM = 4, 64
DIL_ROT_DIM = DIL_HEAD_DIM // 4
DIL_SPAN = 128
ROPE_THETA = 500000.0
XATTN_HEADS, XATTN_HEAD_DIM = 4, 128
FFN_CONV = 3
N_BRANCH = 3

RET_QK_W = RET_HEADS * RET_QK_DIM
RET_V_W = RET_HEADS * RET_V_DIM
GDN_W = GDN_HEADS * GDN_HEAD_DIM
DIL_GW = DIL_HEADS * DIL_HEAD_DIM
DIL_W = len(DIL_GROUPS) * DIL_GW
XATTN_W = XATTN_HEADS * XATTN_HEAD_DIM

LANES = 128
SUBLANES = 8
VMEM_LIMIT_BYTES = 56 * 1024 * 1024

COL_GATES = 0
COL_RQ = COL_GATES + N_BRANCH * 1024
COL_RK = COL_RQ + RET_QK_W
COL_RV = COL_RK + RET_QK_W
COL_RG = COL_RV + RET_V_W
COL_GQKV = COL_RG + RET_V_W
COL_GZ = COL_GQKV + 3 * GDN_W
COL_DQ = COL_GZ + GDN_W
COL_DK = COL_DQ + DIL_W
COL_DV = COL_DK + DIL_W
PROJ_W = COL_DV + DIL_W
GAB_PAD = LANES


def _params(*sem):
    return pltpu.CompilerParams(dimension_semantics=sem, vmem_limit_bytes=VMEM_LIMIT_BYTES)


def _rms(x, g):
    return x * lax.rsqrt(jnp.mean(x * x, axis=-1, keepdims=True) + EPS) * g


def _dot(a, b):
    return jnp.dot(a, b, preferred_element_type=F32)


def _dot_nt(a, b):
    return lax.dot_general(a, b, (((1,), (1,)), ((), ())), preferred_element_type=F32)


def _dot_tn(a, b):
    return lax.dot_general(a, b, (((0,), (0,)), ((), ())), preferred_element_type=F32)


def _silu(x):
    return x * jax.nn.sigmoid(x)


def _lane_sum_bcast(x):
    hi = x.astype(BF16)
    lo = (x - hi.astype(F32)).astype(BF16)
    ones = jnp.ones((2 * LANES, LANES), BF16)
    return _dot(jnp.concatenate([hi, lo], axis=1), ones)


def _iota2(shape, dim):
    return lax.broadcasted_iota(jnp.int32, shape, dim)


def _in_proj_kernel(x_ref, g_ref, w_ref, wab_ref, wabt_ref, proj_ref, gab_ref, gabt_ref, h_scr, *, tn):
    h_scr[...] = _rms(x_ref[...], g_ref[...]).astype(BF16)
    for j in range(w_ref.shape[1] // tn):
        cols = slice(j * tn, (j + 1) * tn)
        proj_ref[:, cols] = _dot(h_scr[...], w_ref[:, cols]).astype(BF16)
    gab_ref[...] = _dot(h_scr[...], wab_ref[...])
    gabt_ref[...] = _dot_nt(wabt_ref[...], h_scr[...])


def _in_proj(x2, g, w_main, w_ab, w_abt, *, tm, tn):
    m, d = x2.shape
    n = w_main.shape[1]
    resident = lambda a: pl.BlockSpec(a.shape, lambda i: (0, 0), pipeline_mode=pl.Buffered(1))
    return pl.pallas_call(
        functools.partial(_in_proj_kernel, tn=tn),
        grid=(m // tm,),
        in_specs=[
            pl.BlockSpec((tm, d), lambda i: (i, 0)),
            resident(g), resident(w_main), resident(w_ab), resident(w_abt),
        ],
        out_specs=[
            pl.BlockSpec((tm, n), lambda i: (i, 0)),
            pl.BlockSpec((tm, GAB_PAD), lambda i: (i, 0)),
            pl.BlockSpec((SUBLANES, tm), lambda i: (0, i)),
        ],
        out_shape=[
            jax.ShapeDtypeStruct((m, n), BF16),
            jax.ShapeDtypeStruct((m, GAB_PAD), F32),
            jax.ShapeDtypeStruct((SUBLANES, m), F32),
        ],
        scratch_shapes=[pltpu.VMEM((tm, d), BF16)],
        compiler_params=_params("parallel"),
        name="in_proj",
    )(x2, g, w_main, w_ab, w_abt)


def _norm_matmul_kernel(x_ref, g_ref, w_ref, o_ref):
    hb = _rms(x_ref[...], g_ref[...]).astype(BF16)
    o_ref[...] = _dot(hb, w_ref[...]).astype(BF16)


def _norm_matmul(x2, g, w, *, tm):
    m, d = x2.shape
    n = w.shape[1]
    return pl.pallas_call(
        _norm_matmul_kernel,
        grid=(m // tm,),
        in_specs=[
            pl.BlockSpec((tm, d), lambda i: (i, 0)),
            pl.BlockSpec((1, d), lambda i: (0, 0)),
            pl.BlockSpec((d, n), lambda i: (0, 0)),
        ],
        out_specs=pl.BlockSpec((tm, n), lambda i: (i, 0)),
        out_shape=jax.ShapeDtypeStruct((m, n), BF16),
        compiler_params=_params("parallel"),
        name="norm_matmul",
    )(x2, g, w)


def _ret_kernel(cs_ref, sn_ref, q_ref, k_ref, v_ref, gate_ref, ng_ref, o_ref, state):
    t = q_ref.shape[1]
    c = RET_CHUNK
    nh = RET_HEADS
    state[...] = jnp.zeros_like(state)

    lane = _iota2((1, RET_QK_W), 1)
    first_half = (lane % RET_QK_DIM) < (RET_QK_DIM // 2)
    head_of_lane = lane // RET_QK_DIM
    diff = (_iota2((c, c), 0) - _iota2((c, c), 1)).astype(F32)
    idx_col = _iota2((c, 1), 0).astype(F32)
    idx_row = _iota2((1, c), 1).astype(F32)
    log_gamma = [math.log1p(-(2.0 ** (-5.0 - h))) for h in range(nh)]
    decay_all = jnp.concatenate(
        [jnp.where(diff >= 0, jnp.exp(jnp.maximum(diff, 0.0) * lg), 0.0) for lg in log_gamma], axis=0)
    q_decay_all = jnp.concatenate([jnp.exp((idx_col + 1.0) * lg) for lg in log_gamma], axis=0)
    k_decay = [jnp.exp((c - 1.0 - idx_row) * lg) for lg in log_gamma]
    chunk_decay = [math.exp(c * lg) for lg in log_gamma]
    ng_all = jnp.concatenate(
        [jnp.broadcast_to(ng_ref[h:h + 1, :], (c, RET_V_DIM)) for h in range(nh)], axis=0)

    def rope(z, cs, sn):
        partner = jnp.where(first_half, -pltpu.roll(z, RET_QK_W - RET_QK_DIM // 2, 1),
                            pltpu.roll(z, RET_QK_DIM // 2, 1))
        return z * cs + partner * sn

    def chunk(ci, carry):
        r0 = pl.multiple_of(ci * c, c)
        rows = pl.ds(r0, c)
        cs, sn = cs_ref[0, rows, :], sn_ref[0, rows, :]
        cs = jnp.concatenate([cs, cs], axis=1)
        sn = jnp.concatenate([sn, sn], axis=1)
        q = rope(q_ref[0, rows, :].astype(F32), cs, sn)
        k = rope(k_ref[0, rows, :].astype(F32), cs, sn) * (RET_QK_DIM ** -0.5)
        k_t = k.T
        k_t_bf = k_t.astype(BF16)
        q_st = jnp.concatenate([jnp.where(head_of_lane == h, q, 0.0) for h in range(nh)], axis=0)
        s_bf = (_dot(q_st.astype(BF16), k_t_bf) * decay_all).astype(BF16)
        qd_bf = (q_st * q_decay_all).astype(BF16)
        vs = [v_ref[0, rows, h * RET_V_DIM:(h + 1) * RET_V_DIM] for h in range(nh)]
        sts = [state[h] for h in range(nh)]
        intra = [_dot(s_bf[h * c:(h + 1) * c], vs[h]) for h in range(nh)]
        inter = [_dot(qd_bf[h * c:(h + 1) * c], sts[h].astype(BF16)) for h in range(nh)]
        upd = [_dot((k_t * k_decay[h]).astype(BF16), vs[h]) for h in range(nh)]
        for h in range(nh):
            state[h] = sts[h] * chunk_decay[h] + upd[h]
        o_all = jnp.concatenate([intra[h] + inter[h] for h in range(nh)], axis=0)
        ss = _lane_sum_bcast(o_all * o_all)
        y_all = o_all * lax.rsqrt(ss * (1.0 / RET_V_DIM) + EPS) * ng_all
        for h in range(nh):
            cols = slice(h * RET_V_DIM, (h + 1) * RET_V_DIM)
            y = y_all[h * c:(h + 1) * c] * _silu(gate_ref[0, rows, cols].astype(F32))
            o_ref[0, rows, cols] = y.astype(BF16)
        return carry

    lax.fori_loop(0, t // c, chunk, 0)


def _retention(proj3, ret_cs, ret_sn, norm_g):
    b, t, _ = proj3.shape
    return pl.pallas_call(
        _ret_kernel,
        grid=(b,),
        in_specs=[
            pl.BlockSpec((1, t, LANES), lambda i: (i, 0, 0)),
            pl.BlockSpec((1, t, LANES), lambda i: (i, 0, 0)),
            pl.BlockSpec((1, t, RET_QK_W), lambda i: (i, 0, COL_RQ // RET_QK_W)),
            pl.BlockSpec((1, t, RET_QK_W), lambda i: (i, 0, COL_RK // RET_QK_W)),
            pl.BlockSpec((1, t, RET_V_W), lambda i: (i, 0, COL_RV // RET_V_W)),
            pl.BlockSpec((1, t, RET_V_W), lambda i: (i, 0, COL_RG // RET_V_W)),
            pl.BlockSpec((RET_HEADS, RET_V_DIM), lambda i: (0, 0)),
        ],
        out_specs=pl.BlockSpec((1, t, RET_V_W), lambda i: (i, 0, 0)),
        out_shape=jax.ShapeDtypeStruct((b, t, RET_V_W), BF16),
        scratch_shapes=[pltpu.VMEM((RET_HEADS, RET_QK_W, RET_V_DIM), F32)],
        compiler_params=_params("parallel"),
        name="retention",
    )(ret_cs, ret_sn, proj3, proj3, proj3, proj3, norm_g)


GDN_CHUNKS_PER_STEP = 4


def _softplus(x):
    return jnp.maximum(x, 0.0) + jnp.log1p(jnp.exp(-jnp.abs(x)))


def _gdn_kernel(qkv_ref, z_ref, gab_ref, gabt_ref, cw_ref, alog_l_ref, dt_l_ref, alog_r_ref,
                dt_r_ref, ng_ref, o_ref, state):
    t = qkv_ref.shape[1]
    c = GDN_CHUNK
    hd = GDN_HEAD_DIM
    nh = GDN_HEADS
    m = nh * c
    win = 2 * c
    state[...] = jnp.zeros_like(state)

    row, col = _iota2((m, m), 0), _iota2((m, m), 1)
    same_head = (row // c) == (col // c)
    tri_bd = same_head & (row >= col)
    strict_bd = same_head & (row > col)
    triu_bd = (same_head & (row <= col)).astype(F32)
    tril_c = (_iota2((c, c), 0) >= _iota2((c, c), 1)).astype(F32)
    n_shifted = GDN_CONV - 1
    sel_row, sel_col = _iota2((n_shifted * c, win), 0), _iota2((n_shifted * c, win), 1)
    sel_target = (sel_row % c) + (sel_row // c) - n_shifted
    scale = hd ** -0.5
    n_doublings = int(math.log2(c)) - 1
    ng_all = jnp.broadcast_to(ng_ref[...], (m, hd))

    def stack_heads(y, base):
        return jnp.concatenate([y[:, base + h * hd:base + (h + 1) * hd] for h in range(nh)], axis=0)

    def stack_cols(a, first):
        return jnp.concatenate([a[:, first + h:first + h + 1] for h in range(nh)], axis=0)

    def prepare(ci):
        r0 = pl.multiple_of(ci * c, c)
        rows = pl.ds(r0, c)
        start = pl.multiple_of(jnp.maximum(r0 - c, 0), c)
        shift = jnp.where(ci > 0, c, 0)
        sel = (sel_col == sel_target + shift).astype(BF16)
        shifted = _dot(sel, qkv_ref[0, pl.ds(start, win), :])
        y = cw_ref[n_shifted:GDN_CONV, :] * qkv_ref[0, rows, :].astype(F32)
        for j in range(n_shifted):
            y = y + cw_ref[j:j + 1, :] * shifted[j * c:(j + 1) * c]
        y = _silu(y)
        q, k, v = stack_heads(y, 0), stack_heads(y, GDN_W), stack_heads(y, 2 * GDN_W)

        gab = gab_ref[rows, :]
        g_col = -jnp.exp(alog_l_ref[...]) * _softplus(gab + dt_l_ref[...])
        gc_col = jnp.dot(tril_c, g_col, precision=lax.Precision.HIGHEST,
                         preferred_element_type=F32)
        beta = stack_cols(jax.nn.sigmoid(gab), nh)
        gcc = stack_cols(gc_col, 0)
        g_last = [gc_col[c - 1:c, h:h + 1] for h in range(nh)]
        g_last_all = jnp.concatenate([jnp.broadcast_to(g, (c, 1)) for g in g_last], axis=0)
        g_row = -jnp.exp(alog_r_ref[...]) * _softplus(gabt_ref[ci] + dt_r_ref[...])
        gcr = jnp.dot(jnp.broadcast_to(g_row, (SUBLANES, m)), triu_bd,
                      precision=lax.Precision.HIGHEST, preferred_element_type=F32)[0:1, :]
        decay = jnp.where(tri_bd, jnp.exp(jnp.where(tri_bd, gcc - gcr, 0.0)), 0.0)

        q = q * lax.rsqrt(_lane_sum_bcast(q * q) + EPS) * scale
        k = k * lax.rsqrt(_lane_sum_bcast(k * k) + EPS)
        k_t_bf = k.T.astype(BF16)
        kb = k * beta
        eg = jnp.exp(gcc)
        raw = _dot(jnp.concatenate([kb, q], axis=0).astype(BF16), k_t_bf)
        n_mat = jnp.where(strict_bd, raw[0:m] * decay, 0.0)
        attn = jnp.where(tri_bd, raw[m:2 * m] * decay, 0.0)
        x = jnp.concatenate([v * beta, kb * eg], axis=1)
        return dict(rows=rows, n_pow=n_mat, x=x, attn_bf=attn.astype(BF16),
                    qg_bf=(q * eg).astype(BF16), kg_bf=(k * jnp.exp(g_last_all - gcc)).astype(BF16),
                    g_last=g_last)

    def step(si, carry):
        chunks = [prepare(si * GDN_CHUNKS_PER_STEP + j) for j in range(GDN_CHUNKS_PER_STEP)]
        for d in chunks:
            d["x"] = d["x"] - _dot(d["n_pow"].astype(BF16), d["x"].astype(BF16))
        for _ in range(n_doublings):
            for d in chunks:
                p_bf = d["n_pow"].astype(BF16)
                d["n_pow"] = _dot(p_bf, p_bf)
            for d in chunks:
                d["x"] = d["x"] + _dot(d["n_pow"].astype(BF16), d["x"].astype(BF16))
        for d in chunks:
            u, w_bf = d["x"][:, :hd], d["x"][:, hd:].astype(BF16)
            sts = [state[h] for h in range(nh)]
            sts_bf = [s.astype(BF16) for s in sts]
            hrows = [slice(h * c, (h + 1) * c) for h in range(nh)]
            v_new = [u[hrows[h]] - _dot(w_bf[hrows[h]], sts_bf[h]) for h in range(nh)]
            o_state = [_dot(d["qg_bf"][hrows[h]], sts_bf[h]) for h in range(nh)]
            v_new_bf = [vn.astype(BF16) for vn in v_new]
            for h in range(nh):
                state[h] = sts[h] * jnp.exp(d["g_last"][h]) + _dot_tn(d["kg_bf"][hrows[h]], v_new_bf[h])
            o = jnp.concatenate(o_state, axis=0) + _dot(d["attn_bf"], jnp.concatenate(v_new_bf, axis=0))
            y = o * lax.rsqrt(_lane_sum_bcast(o * o) * (1.0 / hd) + EPS) * ng_all
            for h in range(nh):
                cols = slice(h * hd, (h + 1) * hd)
                out = y[hrows[h]] * _silu(z_ref[0, d["rows"], cols].astype(F32))
                o_ref[0, d["rows"], cols] = out.astype(BF16)
        return carry

    lax.fori_loop(0, t // c // GDN_CHUNKS_PER_STEP, step, 0)


def _gdn(proj3, gab, gabt_rows, conv_w, alog_l, dt_l, alog_r, dt_r, norm_g):
    b, t, _ = proj3.shape
    m = GDN_HEADS * GDN_CHUNK
    return pl.pallas_call(
        _gdn_kernel,
        grid=(b,),
        in_specs=[
            pl.BlockSpec((1, t, 3 * GDN_W), lambda i: (i, 0, COL_GQKV // (3 * GDN_W))),
            pl.BlockSpec((1, t, GDN_W), lambda i: (i, 0, COL_GZ // GDN_W)),
            pl.BlockSpec((t, GAB_PAD), lambda i: (i, 0)),
            pl.BlockSpec((t // GDN_CHUNK, 1, m), lambda i: (i, 0, 0)),
            pl.BlockSpec((GDN_CONV, 3 * GDN_W), lambda i: (0, 0)),
            pl.BlockSpec((1, GAB_PAD), lambda i: (0, 0)),
            pl.BlockSpec((1, GAB_PAD), lambda i: (0, 0)),
            pl.BlockSpec((1, m), lambda i: (0, 0)),
            pl.BlockSpec((1, m), lambda i: (0, 0)),
            pl.BlockSpec((1, GDN_HEAD_DIM), lambda i: (0, 0)),
        ],
        out_specs=pl.BlockSpec((1, t, GDN_W), lambda i: (i, 0, 0)),
        out_shape=jax.ShapeDtypeStruct((b, t, GDN_W), BF16),
        scratch_shapes=[pltpu.VMEM((GDN_HEADS, GDN_HEAD_DIM, GDN_HEAD_DIM), F32)],
        compiler_params=_params("parallel"),
        name="gdn",
    )(proj3, proj3, gab, gabt_rows, conv_w, alog_l, dt_l, alog_r, dt_r, norm_g)


def _rope_tables_kernel(pos_ref, ret_inv_ref, dil_inv_ref, ret_cs_ref, ret_sn_ref, dil_cs_ref,
                        dil_sn_ref):
    pos = pos_ref[0]
    ang = pos * ret_inv_ref[...]
    ret_cs_ref[0] = jnp.cos(ang)
    ret_sn_ref[0] = jnp.sin(ang)
    ang = pos * dil_inv_ref[...]
    dil_cs_ref[0] = jnp.cos(ang)
    dil_sn_ref[0] = jnp.sin(ang)


def _rope_tables(pos3, ret_inv_row, dil_inv_row):
    b, t, _ = pos3.shape
    table = jax.ShapeDtypeStruct((b, t, LANES), F32)
    spec = pl.BlockSpec((1, t, LANES), lambda i: (i, 0, 0))
    return pl.pallas_call(
        _rope_tables_kernel,
        grid=(b,),
        in_specs=[
            pl.BlockSpec((1, t, 1), lambda i: (i, 0, 0)),
            pl.BlockSpec((1, LANES), lambda i: (0, 0)),
            pl.BlockSpec((1, LANES), lambda i: (0, 0)),
        ],
        out_specs=[spec, spec, spec, spec],
        out_shape=[table, table, table, table],
        compiler_params=_params("parallel"),
        name="rope_tables",
    )(pos3, ret_inv_row, dil_inv_row)


DIL_BLOCKS_PER_STEP = 4


def _split_store(ref, lead, rows, val):
    for half in range(val.shape[1] // LANES):
        ref[lead + (half, rows, slice(None))] = val[:, half * LANES:(half + 1) * LANES]


def _split_load(ref, lead, rows):
    return jnp.concatenate([ref[lead + (half, rows, slice(None))] for half in range(DIL_GW // LANES)],
                           axis=1)


def _dil_group(g, dilation, qs, ks, vs, o_scr, lse_scr):
    t = qs.shape[1]
    span = DIL_SPAN
    nh = DIL_HEADS
    nb = t // dilation // span
    nk = 2 * span if nb > 1 else span
    head_of_lane = _iota2((1, DIL_GW), 1) // DIL_HEAD_DIM
    iq = _iota2((nh * span, nk), 0) % span
    ik = _iota2((nh * span, nk), 1)
    if nb > 1:
        dist = span + iq - ik
        band = (dist >= 0) & (dist <= span)
    else:
        band = iq >= ik

    def rows_of(r, n):
        if dilation == 1:
            return pl.ds(pl.multiple_of(span * n, span), span)
        return pl.ds(r + dilation * span * n, span, stride=dilation)

    def scores(r, n):
        cur = rows_of(r, n)
        qb = _split_load(qs, (), cur)
        if nb > 1:
            prev = rows_of(r, jnp.maximum(n - 1, 0) if dilation == 1 else max(n - 1, 0))
            kk = jnp.concatenate([_split_load(ks, (), prev), _split_load(ks, (), cur)], axis=0)
            vv = jnp.concatenate([_split_load(vs, (), prev), _split_load(vs, (), cur)], axis=0)
            valid = band & (ik >= jnp.where(n > 0, 0, span))
        else:
            kk, vv, valid = _split_load(ks, (), cur), _split_load(vs, (), cur), band
        q_st = jnp.concatenate([jnp.where(head_of_lane == h, qb, 0.0) for h in range(nh)], axis=0)
        s = _dot_nt(q_st.astype(BF16), kk.astype(BF16))
        s = jnp.where(valid, s, NEG_INF)
        m = jnp.max(s, axis=-1, keepdims=True)
        p = jnp.exp(s - m)
        den = jnp.sum(p, axis=-1, keepdims=True)
        pn = (p * (1.0 / den)).astype(BF16)
        return cur, pn, vv, m + jnp.log(den)

    def outputs(cur, pn, vv, lse):
        p_cat = jnp.concatenate([pn[h * span:(h + 1) * span] for h in range(nh)], axis=1)
        v_st = jnp.concatenate([jnp.where(head_of_lane == h, vv, 0.0) for h in range(nh)],
                               axis=0).astype(BF16)
        lse_x = jnp.zeros((span, DIL_GW), F32)
        for h in range(nh):
            lse_x = jnp.where(head_of_lane == h, lse[h * span:(h + 1) * span], lse_x)
        _split_store(o_scr, (g,), cur, _dot(p_cat, v_st))
        _split_store(lse_scr, (g,), cur, lse_x)

    def run(blocks):
        staged = [scores(r, n) for r, n in blocks]
        for args in staged:
            outputs(*args)

    step = min(DIL_BLOCKS_PER_STEP, nb * dilation)
    if dilation == 1:
        def body(i, carry):
            run([(0, i * step + j) for j in range(step)])
            return carry
        lax.fori_loop(0, nb // step, body, 0)
    else:
        blocks = [(r, n) for r in range(dilation) for n in range(nb)]
        for i in range(0, len(blocks), step):
            run(blocks[i:i + step])


def _dil_kernel(cs_ref, sn_ref, q_ref, k_ref, v_ref, o_ref, s1_scr, s2_scr, qs, ks, vs, o_scr, lse_scr):
    g = pl.program_id(1)
    half = DIL_ROT_DIM // 2

    @pl.when(g == 0)
    def _():
        lane = _iota2((1, LANES), 1) % DIL_HEAD_DIM
        sn = sn_ref[0]
        s1_scr[...] = jnp.where(lane < half, -sn, 0.0)
        s2_scr[...] = jnp.where((lane >= half) & (lane < DIL_ROT_DIM), sn, 0.0)

    def both(a):
        return jnp.concatenate([a, a], axis=1)

    def rope(z):
        return (z * both(cs_ref[0]) + pltpu.roll(z, DIL_GW - half, 1) * both(s1_scr[...])
                + pltpu.roll(z, half, 1) * both(s2_scr[...]))

    everything = slice(None)
    _split_store(qs, (), everything, rope(q_ref[0].astype(F32)) * (DIL_HEAD_DIM ** -0.5))
    _split_store(ks, (), everything, rope(k_ref[0].astype(F32)))
    _split_store(vs, (), everything, v_ref[0].astype(F32))

    for gi, (window, dilation) in enumerate(DIL_GROUPS):
        assert window // dilation == DIL_SPAN

        @pl.when(g == gi)
        def _(gi=gi, dilation=dilation):
            _dil_group(gi, dilation, qs, ks, vs, o_scr, lse_scr)

    @pl.when(g == len(DIL_GROUPS) - 1)
    def _():
        for half_i in range(DIL_GW // LANES):
            lses = [lse_scr[i, half_i] for i in range(len(DIL_GROUPS))]
            m = functools.reduce(jnp.maximum, lses)
            es = [jnp.exp(l - m) for l in lses]
            den = functools.reduce(lambda a, b: a + b, es)
            num = functools.reduce(lambda a, b: a + b,
                                   [e * o_scr[i, half_i] for i, e in enumerate(es)])
            o_ref[0, :, half_i * LANES:(half_i + 1) * LANES] = (num / den).astype(BF16)


def _dilated(proj3, dil_cs, dil_sn):
    b, t, _ = proj3.shape
    ng = len(DIL_GROUPS)
    table = pltpu.VMEM((t, LANES), F32)
    split = pltpu.VMEM((DIL_GW // LANES, t, LANES), F32)
    return pl.pallas_call(
        _dil_kernel,
        grid=(b, ng),
        in_specs=[
            pl.BlockSpec((1, t, LANES), lambda i, g: (i, 0, 0)),
            pl.BlockSpec((1, t, LANES), lambda i, g: (i, 0, 0)),
            pl.BlockSpec((1, t, DIL_GW), lambda i, g: (i, 0, COL_DQ // DIL_GW + g)),
            pl.BlockSpec((1, t, DIL_GW), lambda i, g: (i, 0, COL_DK // DIL_GW + g)),
            pl.BlockSpec((1, t, DIL_GW), lambda i, g: (i, 0, COL_DV // DIL_GW + g)),
        ],
        out_specs=pl.BlockSpec((1, t, DIL_GW), lambda i, g: (i, 0, 0)),
        out_shape=jax.ShapeDtypeStruct((b, t, DIL_GW), BF16),
        scratch_shapes=[table, table, split, split, split,
                        pltpu.VMEM((ng, DIL_GW // LANES, t, LANES), F32),
                        pltpu.VMEM((ng, DIL_GW // LANES, t, LANES), F32)],
        compiler_params=_params("parallel", "arbitrary"),
        name="dilated",
    )(dil_cs, dil_sn, proj3, proj3, proj3)


def _merge_kernel(x_ref, gates_ref, oret_ref, ogdn_ref, odil_ref, wr_ref, wg_ref, wd_ref, wo_ref,
                  out_ref):
    d = x_ref.shape[1]
    branches = (_dot(oret_ref[...], wr_ref[...]), _dot(ogdn_ref[...], wg_ref[...]),
                _dot(odil_ref[...], wd_ref[...]))
    merged = None
    for bi, br in enumerate(branches):
        term = jax.nn.sigmoid(gates_ref[:, bi * d:(bi + 1) * d].astype(F32)) * br
        merged = term if merged is None else merged + term
    out_ref[...] = x_ref[...] + _dot(merged.astype(BF16), wo_ref[...])


def _merge(x2, proj2, o_ret, o_gdn, o_dil, w_ret, w_gdn, w_dil, w_out, *, tm):
    m, d = x2.shape
    full = lambda a: pl.BlockSpec(a.shape, lambda i: (0, 0))
    return pl.pallas_call(
        _merge_kernel,
        grid=(m // tm,),
        in_specs=[
            pl.BlockSpec((tm, d), lambda i: (i, 0)),
            pl.BlockSpec((tm, N_BRANCH * d), lambda i: (i, COL_GATES // (N_BRANCH * d))),
            pl.BlockSpec((tm, RET_V_W), lambda i: (i, 0)),
            pl.BlockSpec((tm, GDN_W), lambda i: (i, 0)),
            pl.BlockSpec((tm, DIL_GW), lambda i: (i, 0)),
            full(w_ret), full(w_gdn), full(w_dil), full(w_out),
        ],
        out_specs=pl.BlockSpec((tm, d), lambda i: (i, 0)),
        out_shape=jax.ShapeDtypeStruct((m, d), F32),
        compiler_params=_params("parallel"),
        name="merge",
    )(x2, proj2, o_ret, o_gdn, o_dil, w_ret, w_gdn, w_dil, w_out)


def _xattn_kernel(x_ref, g_ref, kv_ref, wq_ref, wo_ref, out_ref):
    x = x_ref[0]
    hb = _rms(x, g_ref[...]).astype(BF16)
    q = _dot(hb, wq_ref[...])
    outs = []
    for h in range(XATTN_HEADS):
        cols = slice(h * XATTN_HEAD_DIM, (h + 1) * XATTN_HEAD_DIM)
        kh = kv_ref[0, :, cols]
        vh = kv_ref[0, :, XATTN_W + h * XATTN_HEAD_DIM:XATTN_W + (h + 1) * XATTN_HEAD_DIM]
        s = _dot_nt(q[:, cols].astype(BF16), kh) * (XATTN_HEAD_DIM ** -0.5)
        e = jnp.exp(s - jnp.max(s, axis=-1, keepdims=True))
        p = e / jnp.sum(e, axis=-1, keepdims=True)
        outs.append(_dot(p.astype(BF16), vh))
    o = jnp.concatenate(outs, axis=1).astype(BF16)
    out_ref[0] = x + _dot(o, wo_ref[...])


def _xattn(x3, g, kv3, wq, wo, *, tm):
    b, t, d = x3.shape
    mem = kv3.shape[1]
    return pl.pallas_call(
        _xattn_kernel,
        grid=(b, t // tm),
        in_specs=[
            pl.BlockSpec((1, tm, d), lambda i, j: (i, j, 0)),
            pl.BlockSpec((1, d), lambda i, j: (0, 0)),
            pl.BlockSpec((1, mem, 2 * XATTN_W), lambda i, j: (i, 0, 0)),
            pl.BlockSpec(wq.shape, lambda i, j: (0, 0)),
            pl.BlockSpec(wo.shape, lambda i, j: (0, 0)),
        ],
        out_specs=pl.BlockSpec((1, tm, d), lambda i, j: (i, j, 0)),
        out_shape=jax.ShapeDtypeStruct((b, t, d), F32),
        compiler_params=_params("parallel", "parallel"),
        name="xattn",
    )(x3, g, kv3, wq, wo)


def _ffn_kernel(x_ref, g_ref, wup_ref, cw_ref, cb_ref, wd_ref, fg_ref, out_ref,
                h_scr, act_scr, a_scr, u_scr, hist_a, hist_u, *, tf, final_norm):
    tm = x_ref.shape[1]
    ffn_dim = wd_ref.shape[0]
    halo = SUBLANES

    @pl.when(pl.program_id(1) == 0)
    def _():
        hist_a[...] = jnp.zeros_like(hist_a)
        hist_u[...] = jnp.zeros_like(hist_u)

    x = x_ref[0]
    h_scr[...] = _rms(x, g_ref[...]).astype(BF16)

    def up_conv(scr, hist, c, col0):
        cols = slice(col0, col0 + tf)
        up = _dot(h_scr[...], wup_ref[:, cols])
        scr[0:halo, :] = hist[c]
        scr[halo:, :] = up
        hist[c] = up[tm - halo:tm]
        acc = cb_ref[:, cols]
        for j in range(FFN_CONV):
            off = halo - (FFN_CONV - 1) + j
            acc = acc + cw_ref[j:j + 1, cols] * scr[off:off + tm, :]
        return acc

    for c in range(ffn_dim // tf):
        a = up_conv(a_scr, hist_a, c, c * tf)
        u = up_conv(u_scr, hist_u, c, ffn_dim + c * tf)
        act_scr[:, c * tf:(c + 1) * tf] = (_silu(a) * u).astype(BF16)

    out = x + _dot(act_scr[...], wd_ref[...])
    if final_norm:
        out = _rms(out, fg_ref[...])
    out_ref[0] = out


def _ffn(x3, g, w_up, conv_w, conv_b, w_down, final_g, *, tm, tf, final_norm):
    b, t, d = x3.shape
    ffn_dim = w_down.shape[0]
    nf = ffn_dim // tf
    resident = lambda a: pl.BlockSpec(a.shape, lambda i, j: (0, 0), pipeline_mode=pl.Buffered(1))
    return pl.pallas_call(
        functools.partial(_ffn_kernel, tf=tf, final_norm=final_norm),
        grid=(b, t // tm),
        in_specs=[
            pl.BlockSpec((1, tm, d), lambda i, j: (i, j, 0)),
            resident(g), resident(w_up), resident(conv_w), resident(conv_b), resident(w_down),
            resident(final_g),
        ],
        out_specs=pl.BlockSpec((1, tm, d), lambda i, j: (i, j, 0)),
        out_shape=jax.ShapeDtypeStruct((b, t, d), F32),
        scratch_shapes=[
            pltpu.VMEM((tm, d), BF16),
            pltpu.VMEM((tm, ffn_dim), BF16),
            pltpu.VMEM((tm + SUBLANES, tf), F32),
            pltpu.VMEM((tm + SUBLANES, tf), F32),
            pltpu.VMEM((nf, SUBLANES, tf), F32),
            pltpu.VMEM((nf, SUBLANES, tf), F32),
        ],
        compiler_params=_params("parallel", "arbitrary"),
        name="ffn",
    )(x3, g, w_up, conv_w, conv_b, w_down, final_g)


def _pad_lanes(v, width):
    return jnp.pad(v.astype(F32), (0, width - v.shape[0]))[None, :]


def kernel(x, mem, positions, norm_mix_g, w_in, ret_norm_g, gdn_conv_w, gdn_a_log, gdn_dt_bias,
           gdn_norm_g, w_br_ret, w_br_gdn, w_br_dil, w_out, norm_xattn_g, norm_mem_g, xattn_wq,
           xattn_wkv, xattn_wo, norm_ffn_g, ffn_w_up, ffn_conv_w, ffn_conv_b, ffn_w_down,
           final_norm_g):
    b, t, d = x.shape
    depth = w_in.shape[0]
    mem_tokens = mem.shape[1]
    ab0 = 2 * RET_QK_W + 2 * RET_V_W + 3 * GDN_W
    ab1 = ab0 + 2 * GDN_HEADS

    pos3 = positions.astype(F32)[:, :, None]
    ret_inv = 1.0 / (RET_ROT_BASE ** jnp.linspace(0.0, 1.0, RET_QK_DIM // 2, dtype=F32))
    ret_inv_row = jnp.tile(ret_inv, LANES // ret_inv.shape[0])[None, :]
    dil_inv = ROPE_THETA ** (-jnp.arange(0, DIL_ROT_DIM, 2, dtype=F32) / DIL_ROT_DIM)
    dil_head = jnp.concatenate([dil_inv, dil_inv, jnp.zeros((DIL_HEAD_DIM - DIL_ROT_DIM,), F32)])
    dil_inv_row = jnp.tile(dil_head, LANES // DIL_HEAD_DIM)[None, :]
    ret_cs, ret_sn, dil_cs, dil_sn = _rope_tables(pos3, ret_inv_row, dil_inv_row)

    mem2 = mem.reshape(b * mem_tokens, d)
    for l in range(depth):
        wl = w_in[l]
        gates_w = wl[:, wl.shape[1] - N_BRANCH * d:]
        w_main = jnp.concatenate([gates_w, wl[:, :ab0], wl[:, ab1:wl.shape[1] - N_BRANCH * d]],
                                 axis=1).astype(BF16)
        w_ab = jnp.pad(wl[:, ab0:ab1], ((0, 0), (0, GAB_PAD - (ab1 - ab0)))).astype(BF16)
        w_abt = wl[:, ab0:ab1].T.astype(BF16)
        alog_l = _pad_lanes(gdn_a_log[l], GAB_PAD)
        dt_l = _pad_lanes(gdn_dt_bias[l], GAB_PAD)
        alog_r = jnp.repeat(gdn_a_log[l].astype(F32), GDN_CHUNK)[None, :]
        dt_r = jnp.repeat(gdn_dt_bias[l].astype(F32), GDN_CHUNK)[None, :]

        x2 = x.reshape(b * t, d)
        proj, gab, gabt = _in_proj(x2, norm_mix_g[l][None, :], w_main, w_ab, w_abt, tm=512, tn=1280)
        proj3 = proj.reshape(b, t, PROJ_W)
        o_ret = _retention(proj3, ret_cs, ret_sn, ret_norm_g[l])
        n_chunks = b * t // GDN_CHUNK
        gabt_rows = (gabt[:GDN_HEADS].reshape(GDN_HEADS, n_chunks, GDN_CHUNK).transpose(1, 0, 2)
                     .reshape(n_chunks, 1, GDN_HEADS * GDN_CHUNK))
        o_gdn = _gdn(proj3, gab, gabt_rows, gdn_conv_w[l], alog_l, dt_l, alog_r, dt_r,
                     gdn_norm_g[l][None, :])
        o_dil = _dilated(proj3, dil_cs, dil_sn)
        x2 = _merge(x2, proj, o_ret.reshape(b * t, RET_V_W), o_gdn.reshape(b * t, GDN_W),
                    o_dil.reshape(b * t, DIL_GW), w_br_ret[l].astype(BF16),
                    w_br_gdn[l].astype(BF16), w_br_dil[l].astype(BF16), w_out[l].astype(BF16),
                    tm=512)

        kv = _norm_matmul(mem2, norm_mem_g[l][None, :], xattn_wkv[l].astype(BF16),
                          tm=mem_tokens)
        x3 = _xattn(x2.reshape(b, t, d), norm_xattn_g[l][None, :],
                    kv.reshape(b, mem_tokens, 2 * XATTN_W), xattn_wq[l].astype(BF16),
                    xattn_wo[l].astype(BF16), tm=512)

        x = _ffn(x3, norm_ffn_g[l][None, :], ffn_w_up[l].astype(BF16), ffn_conv_w[l],
                 ffn_conv_b[l][None, :], ffn_w_down[l].astype(BF16), final_norm_g[None, :],
                 tm=512, tf=256, final_norm=(l == depth - 1))
    return x
```

```python
import functools
import math

import jax
import jax.numpy as jnp
import numpy as np
from jax import lax
from jax.experimental import pallas as pl
from jax.experimental.pallas import tpu as pltpu

F32 = jnp.float32
BF16 = jnp.bfloat16

EPS = 1e-6
NEG_INF = -1e30

RET_HEADS, RET_QK_DIM, RET_V_DIM, RET_CHUNK = 4, 64, 128, 128
RET_ROT_BASE = 10000.0
GDN_HEADS, GDN_HEAD_DIM, GDN_CONV, GDN_CHUNK = 4, 128, 4, 64
DIL_GROUPS = ((128, 1), (512, 4), (2048, 16))
DIL_HEADS, DIL_HEAD_DIM = 4, 64
DIL_ROT_DIM = DIL_HEAD_DIM // 4
DIL_SPAN = 128
ROPE_THETA = 500000.0
XATTN_HEADS, XATTN_HEAD_DIM = 4, 128
FFN_CONV = 3
N_BRANCH = 3

RET_QK_W = RET_HEADS * RET_QK_DIM
RET_V_W = RET_HEADS * RET_V_DIM
GDN_W = GDN_HEADS * GDN_HEAD_DIM
DIL_GW = DIL_HEADS * DIL_HEAD_DIM
DIL_W = len(DIL_GROUPS) * DIL_GW
XATTN_W = XATTN_HEADS * XATTN_HEAD_DIM

LANES = 128
SUBLANES = 8
VMEM_LIMIT_BYTES = 56 * 1024 * 1024

COL_GATES = 0
COL_RQ = COL_GATES + N_BRANCH * 1024
COL_RK = COL_RQ + RET_QK_W
COL_RV = COL_RK + RET_QK_W
COL_RG = COL_RV + RET_V_W
COL_GQKV = COL_RG + RET_V_W
COL_GZ = COL_GQKV + 3 * GDN_W
COL_DQ = COL_GZ + GDN_W
COL_DK = COL_DQ + DIL_W
COL_DV = COL_DK + DIL_W
PROJ_W = COL_DV + DIL_W
GAB_PAD = LANES


def _params(*sem):
    return pltpu.CompilerParams(dimension_semantics=sem, vmem_limit_bytes=VMEM_LIMIT_BYTES)


def _rms(x, g):
    return x * lax.rsqrt(jnp.mean(x * x, axis=-1, keepdims=True) + EPS) * g


def _dot(a, b):
    return jnp.dot(a, b, preferred_element_type=F32)


def _dot_nt(a, b):
    return lax.dot_general(a, b, (((1,), (1,)), ((), ())), preferred_element_type=F32)


def _dot_tn(a, b):
    return lax.dot_general(a, b, (((0,), (0,)), ((), ())), preferred_element_type=F32)


def _silu(x):
    return x * jax.nn.sigmoid(x)


def _lane_sum_bcast(x):
    hi = x.astype(BF16)
    lo = (x - hi.astype(F32)).astype(BF16)
    ones = jnp.ones((2 * LANES, LANES), BF16)
    return _dot(jnp.concatenate([hi, lo], axis=1), ones)


def _iota2(shape, dim):
    return lax.broadcasted_iota(jnp.int32, shape, dim)


def _regroup_w_in_kernel(w_ref, o_ref, *, ab0, ab1):
    n_in = w_ref.shape[2]
    gates0 = n_in - (COL_RQ - COL_GATES)
    o_ref[:, COL_GATES:COL_RQ] = w_ref[0, :, gates0:n_in].astype(BF16)
    o_ref[:, COL_RQ:COL_RQ + ab0] = w_ref[0, :, 0:ab0].astype(BF16)
    o_ref[:, COL_RQ + ab0:PROJ_W] = w_ref[0, :, ab1:gates0].astype(BF16)


def _regroup_w_in(w_in, layer, ab0, ab1, *, tk):
    _, d, n_in = w_in.shape
    return pl.pallas_call(
        functools.partial(_regroup_w_in_kernel, ab0=ab0, ab1=ab1),
        grid=(d // tk,),
        in_specs=[pl.BlockSpec((1, tk, n_in), lambda i: (layer, i, 0))],
        out_specs=pl.BlockSpec((tk, PROJ_W), lambda i: (i, 0)),
        out_shape=jax.ShapeDtypeStruct((d, PROJ_W), BF16),
        compiler_params=_params("parallel"),
        name="regroup_w_in",
    )(w_in)


def _in_proj_kernel(x_ref, g_ref, w_ref, wab_ref, wabt_ref, proj_ref, gab_ref, gabt_ref, h_scr, *, tn):
    h_scr[...] = _rms(x_ref[...], g_ref[...]).astype(BF16)
    for j in range(w_ref.shape[1] // tn):
        cols = slice(j * tn, (j + 1) * tn)
        proj_ref[:, cols] = _dot(h_scr[...], w_ref[:, cols]).astype(BF16)
    gab_ref[...] = _dot(h_scr[...], wab_ref[...])
    gabt_ref[...] = _dot_nt(wabt_ref[...], h_scr[...])


def _in_proj(x2, g, w_main, w_ab, w_abt, *, tm, tn):
    m, d = x2.shape
    n = w_main.shape[1]
    resident = lambda a: pl.BlockSpec(a.shape, lambda i: (0, 0), pipeline_mode=pl.Buffered(1))
    return pl.pallas_call(
        functools.partial(_in_proj_kernel, tn=tn),
        grid=(m // tm,),
        in_specs=[
            pl.BlockSpec((tm, d), lambda i: (i, 0)),
            resident(g), resident(w_main), resident(w_ab), resident(w_abt),
        ],
        out_specs=[
            pl.BlockSpec((tm, n), lambda i: (i, 0)),
            pl.BlockSpec((tm, GAB_PAD), lambda i: (i, 0)),
            pl.BlockSpec((SUBLANES, tm), lambda i: (0, i)),
        ],
        out_shape=[
            jax.ShapeDtypeStruct((m, n), BF16),
            jax.ShapeDtypeStruct((m, GAB_PAD), F32),
            jax.ShapeDtypeStruct((SUBLANES, m), F32),
        ],
        scratch_shapes=[pltpu.VMEM((tm, d), BF16)],
        compiler_params=_params("parallel"),
        name="in_proj",
    )(x2, g, w_main, w_ab, w_abt)


def _norm_matmul_kernel(x_ref, g_ref, w_ref, o_ref):
    hb = _rms(x_ref[...], g_ref[...]).astype(BF16)
    o_ref[...] = _dot(hb, w_ref[...]).astype(BF16)


def _norm_matmul(x2, g, w, *, tm):
    m, d = x2.shape
    n = w.shape[1]
    return pl.pallas_call(
        _norm_matmul_kernel,
        grid=(m // tm,),
        in_specs=[
            pl.BlockSpec((tm, d), lambda i: (i, 0)),
            pl.BlockSpec((1, d), lambda i: (0, 0)),
            pl.BlockSpec((d, n), lambda i: (0, 0)),
        ],
        out_specs=pl.BlockSpec((tm, n), lambda i: (i, 0)),
        out_shape=jax.ShapeDtypeStruct((m, n), BF16),
        compiler_params=_params("parallel"),
        name="norm_matmul",
    )(x2, g, w)


def _ret_kernel(cs_ref, sn_ref, q_ref, k_ref, v_ref, gate_ref, ng_ref, o_ref, state):
    t = q_ref.shape[1]
    c = RET_CHUNK
    nh = RET_HEADS
    state[...] = jnp.zeros_like(state)

    lane = _iota2((1, RET_QK_W), 1)
    first_half = (lane % RET_QK_DIM) < (RET_QK_DIM // 2)
    head_of_lane = lane // RET_QK_DIM
    diff = (_iota2((c, c), 0) - _iota2((c, c), 1)).astype(F32)
    idx_col = _iota2((c, 1), 0).astype(F32)
    idx_row = _iota2((1, c), 1).astype(F32)
    log_gamma = [math.log1p(-(2.0 ** (-5.0 - h))) for h in range(nh)]
    decay_all = jnp.concatenate(
        [jnp.where(diff >= 0, jnp.exp(jnp.maximum(diff, 0.0) * lg), 0.0) for lg in log_gamma], axis=0)
    q_decay_all = jnp.concatenate([jnp.exp((idx_col + 1.0) * lg) for lg in log_gamma], axis=0)
    k_decay = [jnp.exp((c - 1.0 - idx_row) * lg) for lg in log_gamma]
    chunk_decay = [math.exp(c * lg) for lg in log_gamma]
    ng_all = jnp.concatenate(
        [jnp.broadcast_to(ng_ref[h:h + 1, :], (c, RET_V_DIM)) for h in range(nh)], axis=0)

    def rope(z, cs, sn):
        partner = jnp.where(first_half, -pltpu.roll(z, RET_QK_W - RET_QK_DIM // 2, 1),
                            pltpu.roll(z, RET_QK_DIM // 2, 1))
        return z * cs + partner * sn

    def chunk(ci, carry):
        r0 = pl.multiple_of(ci * c, c)
        rows = pl.ds(r0, c)
        cs, sn = cs_ref[0, rows, :], sn_ref[0, rows, :]
        cs = jnp.concatenate([cs, cs], axis=1)
        sn = jnp.concatenate([sn, sn], axis=1)
        q = rope(q_ref[0, rows, :].astype(F32), cs, sn)
        k = rope(k_ref[0, rows, :].astype(F32), cs, sn) * (RET_QK_DIM ** -0.5)
        k_t = k.T
        k_t_bf = k_t.astype(BF16)
        q_st = jnp.concatenate([jnp.where(head_of_lane == h, q, 0.0) for h in range(nh)], axis=0)
        s_bf = (_dot(q_st.astype(BF16), k_t_bf) * decay_all).astype(BF16)
        qd_bf = (q_st * q_decay_all).astype(BF16)
        vs = [v_ref[0, rows, h * RET_V_DIM:(h + 1) * RET_V_DIM] for h in range(nh)]
        sts = [state[h] for h in range(nh)]
        intra = [_dot(s_bf[h * c:(h + 1) * c], vs[h]) for h in range(nh)]
        inter = [_dot(qd_bf[h * c:(h + 1) * c], sts[h].astype(BF16)) for h in range(nh)]
        upd = [_dot((k_t * k_decay[h]).astype(BF16), vs[h]) for h in range(nh)]
        for h in range(nh):
            state[h] = sts[h] * chunk_decay[h] + upd[h]
        o_all = jnp.concatenate([intra[h] + inter[h] for h in range(nh)], axis=0)
        ss = _lane_sum_bcast(o_all * o_all)
        y_all = o_all * lax.rsqrt(ss * (1.0 / RET_V_DIM) + EPS) * ng_all
        for h in range(nh):
            cols = slice(h * RET_V_DIM, (h + 1) * RET_V_DIM)
            y = y_all[h * c:(h + 1) * c] * _silu(gate_ref[0, rows, cols].astype(F32))
            o_ref[0, rows, cols] = y.astype(BF16)
        return carry

    lax.fori_loop(0, t // c, chunk, 0)


def _retention(proj3, ret_cs, ret_sn, norm_g):
    b, t, _ = proj3.shape
    return pl.pallas_call(
        _ret_kernel,
        grid=(b,),
        in_specs=[
            pl.BlockSpec((1, t, LANES), lambda i: (i, 0, 0)),
            pl.BlockSpec((1, t, LANES), lambda i: (i, 0, 0)),
            pl.BlockSpec((1, t, RET_QK_W), lambda i: (i, 0, COL_RQ // RET_QK_W)),
            pl.BlockSpec((1, t, RET_QK_W), lambda i: (i, 0, COL_RK // RET_QK_W)),
            pl.BlockSpec((1, t, RET_V_W), lambda i: (i, 0, COL_RV // RET_V_W)),
            pl.BlockSpec((1, t, RET_V_W), lambda i: (i, 0, COL_RG // RET_V_W)),
            pl.BlockSpec((RET_HEADS, RET_V_DIM), lambda i: (0, 0)),
        ],
        out_specs=pl.BlockSpec((1, t, RET_V_W), lambda i: (i, 0, 0)),
        out_shape=jax.ShapeDtypeStruct((b, t, RET_V_W), BF16),
        scratch_shapes=[pltpu.VMEM((RET_HEADS, RET_QK_W, RET_V_DIM), F32)],
        compiler_params=_params("parallel"),
        name="retention",
    )(ret_cs, ret_sn, proj3, proj3, proj3, proj3, norm_g)


GDN_CHUNKS_PER_STEP = 4
GDN_CHUNKS_PER_BLOCK = 8
GDN_ROWS_PER_STEP = 4


def _softplus(x):
    return jnp.maximum(x, 0.0) + jnp.log1p(jnp.exp(-jnp.abs(x)))


def _gdn_prep_kernel(qkv_ref, gab_ref, gabt_ref, cw_ref, alog_l_ref, dt_l_ref, alog_r_ref, dt_r_ref,
                     u_ref, w_ref, qg_ref, kg_ref, attn_ref, egl_ref):
    c = GDN_CHUNK
    hd = GDN_HEAD_DIM
    nh = GDN_HEADS
    m = nh * c
    win = 2 * c
    n_shifted = GDN_CONV - 1
    block0 = pl.program_id(1) * GDN_CHUNKS_PER_BLOCK
    sel_row, sel_col = _iota2((n_shifted * c, win), 0), _iota2((n_shifted * c, win), 1)
    sel_target = (sel_row % c) + (sel_row // c) - n_shifted

    row, col = _iota2((m, m), 0), _iota2((m, m), 1)
    same_head = (row // c) == (col // c)
    tri_bd = same_head & (row >= col)
    strict_bd = same_head & (row > col)
    triu_bd = (same_head & (row <= col)).astype(F32)
    tril_c = (_iota2((c, c), 0) >= _iota2((c, c), 1)).astype(F32)
    scale = hd ** -0.5
    n_doublings = int(math.log2(c)) - 1

    def stack_heads(y, base):
        return jnp.concatenate([y[:, base + h * hd:base + (h + 1) * hd] for h in range(nh)], axis=0)

    def stack_cols(a, first):
        return jnp.concatenate([a[:, first + h:first + h + 1] for h in range(nh)], axis=0)

    def prepare(j):
        ci = block0 + j
        r0 = pl.multiple_of(ci * c, c)
        rows = pl.ds(r0, c)
        start = pl.multiple_of(jnp.maximum(r0 - c, 0), c)
        shift = jnp.where(ci > 0, c, 0)
        sel = (sel_col == sel_target + shift).astype(BF16)
        shifted = _dot(sel, qkv_ref[0, pl.ds(start, win), :])
        y = cw_ref[n_shifted:GDN_CONV, :] * qkv_ref[0, rows, :].astype(F32)
        for tap in range(n_shifted):
            y = y + cw_ref[tap:tap + 1, :] * shifted[tap * c:(tap + 1) * c]
        y = _silu(y)
        q, k, v = stack_heads(y, 0), stack_heads(y, GDN_W), stack_heads(y, 2 * GDN_W)

        gab = gab_ref[rows, :]
        g_col = -jnp.exp(alog_l_ref[...]) * _softplus(gab + dt_l_ref[...])
        gc_col = jnp.dot(tril_c, g_col, precision=lax.Precision.HIGHEST,
                         preferred_element_type=F32)
        beta = stack_cols(jax.nn.sigmoid(gab), nh)
        gcc = stack_cols(gc_col, 0)
        g_last = [gc_col[c - 1:c, h:h + 1] for h in range(nh)]
        g_last_all = jnp.concatenate([jnp.broadcast_to(g, (c, 1)) for g in g_last], axis=0)
        g_row = -jnp.exp(alog_r_ref[...]) * _softplus(gabt_ref[ci] + dt_r_ref[...])
        gcr = jnp.dot(jnp.broadcast_to(g_row, (SUBLANES, m)), triu_bd,
                      precision=lax.Precision.HIGHEST, preferred_element_type=F32)[0:1, :]
        decay = jnp.where(tri_bd, jnp.exp(jnp.where(tri_bd, gcc - gcr, 0.0)), 0.0)

        q = q * lax.rsqrt(_lane_sum_bcast(q * q) + EPS) * scale
        k = k * lax.rsqrt(_lane_sum_bcast(k * k) + EPS)
        k_t_bf = k.T.astype(BF16)
        kb = k * beta
        eg = jnp.exp(gcc)
        raw = _dot(jnp.concatenate([kb, q], axis=0).astype(BF16), k_t_bf)
        n_mat = jnp.where(strict_bd, raw[0:m] * decay, 0.0)
        attn_ref[0, j] = jnp.where(tri_bd, raw[m:2 * m] * decay, 0.0).astype(BF16)
        qg_ref[0, j] = (q * eg).astype(BF16)
        kg_ref[0, j] = (k * jnp.exp(g_last_all - gcc)).astype(BF16)
        egl_ref[0, j] = jnp.concatenate(
            [jnp.broadcast_to(jnp.exp(g), (1, hd)) for g in g_last]
            + [jnp.zeros((SUBLANES - nh, hd), F32)], axis=0)
        return dict(n_pow=n_mat, x=jnp.concatenate([v * beta, kb * eg], axis=1))

    def step(si, carry):
        first = si * GDN_CHUNKS_PER_STEP
        chunks = [prepare(first + j) for j in range(GDN_CHUNKS_PER_STEP)]
        for d in chunks:
            d["x"] = d["x"] - _dot(d["n_pow"].astype(BF16), d["x"].astype(BF16))
        for _ in range(n_doublings):
            for d in chunks:
                p_bf = d["n_pow"].astype(BF16)
                d["n_pow"] = _dot(p_bf, p_bf)
            for d in chunks:
                d["x"] = d["x"] + _dot(d["n_pow"].astype(BF16), d["x"].astype(BF16))
        for j, d in enumerate(chunks):
            u_ref[0, first + j] = d["x"][:, :hd]
            w_ref[0, first + j] = d["x"][:, hd:].astype(BF16)
        return carry

    lax.fori_loop(0, GDN_CHUNKS_PER_BLOCK // GDN_CHUNKS_PER_STEP, step, 0)


def _gdn_rec_kernel(u_ref, w_ref, qg_ref, kg_ref, attn_ref, egl_ref, z_ref, ng_ref, o_ref, state):
    c = GDN_CHUNK
    hd = GDN_HEAD_DIM
    nh = GDN_HEADS
    nr = u_ref.shape[0]
    hrows = [slice(h * c, (h + 1) * c) for h in range(nh)]
    ng_all = jnp.broadcast_to(ng_ref[...], (nh * c, hd))

    @pl.when(pl.program_id(1) == 0)
    def _():
        state[...] = jnp.zeros_like(state)

    def chunk(j, carry):
        rows = pl.ds(pl.multiple_of(j * c, c), c)
        sts = [[state[r, h] for h in range(nh)] for r in range(nr)]
        sts_bf = [[s.astype(BF16) for s in row] for row in sts]
        w_bf = [w_ref[r, j] for r in range(nr)]
        qg_bf = [qg_ref[r, j] for r in range(nr)]
        v_new = [[u_ref[r, j, hrows[h], :] - _dot(w_bf[r][hrows[h]], sts_bf[r][h]) for h in range(nh)]
                 for r in range(nr)]
        o_state = [[_dot(qg_bf[r][hrows[h]], sts_bf[r][h]) for h in range(nh)] for r in range(nr)]
        v_new_bf = [[vn.astype(BF16) for vn in row] for row in v_new]
        for r in range(nr):
            for h in range(nh):
                state[r, h] = (sts[r][h] * egl_ref[r, j, h:h + 1, :]
                               + _dot_tn(kg_ref[r, j, hrows[h], :], v_new_bf[r][h]))
        for r in range(nr):
            o = (jnp.concatenate(o_state[r], axis=0)
                 + _dot(attn_ref[r, j], jnp.concatenate(v_new_bf[r], axis=0)))
            y = o * lax.rsqrt(_lane_sum_bcast(o * o) * (1.0 / hd) + EPS) * ng_all
            for h in range(nh):
                cols = slice(h * hd, (h + 1) * hd)
                o_ref[r, rows, cols] = (y[hrows[h]] * _silu(z_ref[r, rows, cols].astype(F32))).astype(BF16)
        return carry

    lax.fori_loop(0, u_ref.shape[1], chunk, 0)


def _gdn(proj3, gab, gabt_rows, conv_w, alog_l, dt_l, alog_r, dt_r, norm_g):
    b, t, _ = proj3.shape
    c, hd, nh = GDN_CHUNK, GDN_HEAD_DIM, GDN_HEADS
    m = nh * c
    nc = t // c
    cb = GDN_CHUNKS_PER_BLOCK
    per_chunk = lambda width, dtype: jax.ShapeDtypeStruct((b, nc, m, width), dtype)
    chunk_spec = lambda width: pl.BlockSpec((1, cb, m, width), lambda i, j: (i, j, 0, 0))
    u, w, qg, kg, attn, egl = pl.pallas_call(
        _gdn_prep_kernel,
        grid=(b, nc // cb),
        in_specs=[
            pl.BlockSpec((1, t, 3 * GDN_W), lambda i, j: (i, 0, COL_GQKV // (3 * GDN_W))),
            pl.BlockSpec((t, GAB_PAD), lambda i, j: (i, 0)),
            pl.BlockSpec((nc, 1, m), lambda i, j: (i, 0, 0)),
            pl.BlockSpec((GDN_CONV, 3 * GDN_W), lambda i, j: (0, 0)),
            pl.BlockSpec((1, GAB_PAD), lambda i, j: (0, 0)),
            pl.BlockSpec((1, GAB_PAD), lambda i, j: (0, 0)),
            pl.BlockSpec((1, m), lambda i, j: (0, 0)),
            pl.BlockSpec((1, m), lambda i, j: (0, 0)),
        ],
        out_specs=[chunk_spec(hd), chunk_spec(hd), chunk_spec(hd), chunk_spec(hd), chunk_spec(m),
                   pl.BlockSpec((1, cb, SUBLANES, hd), lambda i, j: (i, j, 0, 0))],
        out_shape=[per_chunk(hd, F32), per_chunk(hd, BF16), per_chunk(hd, BF16), per_chunk(hd, BF16),
                   per_chunk(m, BF16), jax.ShapeDtypeStruct((b, nc, SUBLANES, hd), F32)],
        compiler_params=_params("parallel", "arbitrary"),
        name="gdn_prep",
    )(proj3, gab, gabt_rows, conv_w, alog_l, dt_l, alog_r, dt_r)

    nr = GDN_ROWS_PER_STEP
    rec_spec = lambda width: pl.BlockSpec((nr, cb, m, width), lambda i, j: (i, j, 0, 0))
    return pl.pallas_call(
        _gdn_rec_kernel,
        grid=(b // nr, nc // cb),
        in_specs=[rec_spec(hd), rec_spec(hd), rec_spec(hd), rec_spec(hd), rec_spec(m),
                  pl.BlockSpec((nr, cb, SUBLANES, hd), lambda i, j: (i, j, 0, 0)),
                  pl.BlockSpec((nr, cb * c, GDN_W), lambda i, j: (i, j, COL_GZ // GDN_W)),
                  pl.BlockSpec((1, hd), lambda i, j: (0, 0))],
        out_specs=pl.BlockSpec((nr, cb * c, GDN_W), lambda i, j: (i, j, 0)),
        out_shape=jax.ShapeDtypeStruct((b, t, GDN_W), BF16),
        scratch_shapes=[pltpu.VMEM((nr, nh, hd, hd), F32)],
        compiler_params=_params("parallel", "arbitrary"),
        name="gdn_rec",
    )(u, w, qg, kg, attn, egl, proj3, norm_g)


def _rope_tables_kernel(pos_ref, ret_inv_ref, dil_inv_ref, ret_cs_ref, ret_sn_ref, dil_cs_ref,
                        dil_sn_ref):
    pos = pos_ref[0]
    ang = pos * ret_inv_ref[...]
    ret_cs_ref[0] = jnp.cos(ang)
    ret_sn_ref[0] = jnp.sin(ang)
    ang = pos * dil_inv_ref[...]
    dil_cs_ref[0] = jnp.cos(ang)
    dil_sn_ref[0] = jnp.sin(ang)


def _rope_tables(pos3, ret_inv_row, dil_inv_row):
    b, t, _ = pos3.shape
    table = jax.ShapeDtypeStruct((b, t, LANES), F32)
    spec = pl.BlockSpec((1, t, LANES), lambda i: (i, 0, 0))
    return pl.pallas_call(
        _rope_tables_kernel,
        grid=(b,),
        in_specs=[
            pl.BlockSpec((1, t, 1), lambda i: (i, 0, 0)),
            pl.BlockSpec((1, LANES), lambda i: (0, 0)),
            pl.BlockSpec((1, LANES), lambda i: (0, 0)),
        ],
        out_specs=[spec, spec, spec, spec],
        out_shape=[table, table, table, table],
        compiler_params=_params("parallel"),
        name="rope_tables",
    )(pos3, ret_inv_row, dil_inv_row)


DIL_BLOCKS_PER_STEP = 4


def _split_store(ref, lead, rows, val):
    for half in range(val.shape[1] // LANES):
        ref[lead + (half, rows, slice(None))] = val[:, half * LANES:(half + 1) * LANES]


def _split_load(ref, lead, rows):
    return jnp.concatenate([ref[lead + (half, rows, slice(None))] for half in range(DIL_GW // LANES)],
                           axis=1)


def _dil_group(g, dilation, qs, ks, vs, o_scr, lse_scr):
    t = qs.shape[1]
    span = DIL_SPAN
    nh = DIL_HEADS
    nb = t // dilation // span
    nk = 2 * span if nb > 1 else span
    head_of_lane = _iota2((1, DIL_GW), 1) // DIL_HEAD_DIM
    iq = _iota2((nh * span, nk), 0) % span
    ik = _iota2((nh * span, nk), 1)
    if nb > 1:
        dist = span + iq - ik
        band = (dist >= 0) & (dist <= span)
    else:
        band = iq >= ik

    def rows_of(r, n):
        if dilation == 1:
            return pl.ds(pl.multiple_of(span * n, span), span)
        return pl.ds(r + dilation * span * n, span, stride=dilation)

    def scores(r, n):
        cur = rows_of(r, n)
        qb = _split_load(qs, (), cur)
        if nb > 1:
            prev = rows_of(r, jnp.maximum(n - 1, 0) if dilation == 1 else max(n - 1, 0))
            kk = jnp.concatenate([_split_load(ks, (), prev), _split_load(ks, (), cur)], axis=0)
            vv = jnp.concatenate([_split_load(vs, (), prev), _split_load(vs, (), cur)], axis=0)
            valid = band & (ik >= jnp.where(n > 0, 0, span))
        else:
            kk, vv, valid = _split_load(ks, (), cur), _split_load(vs, (), cur), band
        q_st = jnp.concatenate([jnp.where(head_of_lane == h, qb, 0.0) for h in range(nh)], axis=0)
        s = _dot_nt(q_st.astype(BF16), kk.astype(BF16))
        s = jnp.where(valid, s, NEG_INF)
        m = jnp.max(s, axis=-1, keepdims=True)
        p = jnp.exp(s - m)
        den = jnp.sum(p, axis=-1, keepdims=True)
        pn = (p * (1.0 / den)).astype(BF16)
        return cur, pn, vv, m + jnp.log(den)

    def outputs(cur, pn, vv, lse):
        p_cat = jnp.concatenate([pn[h * span:(h + 1) * span] for h in range(nh)], axis=1)
        v_st = jnp.concatenate([jnp.where(head_of_lane == h, vv, 0.0) for h in range(nh)],
                               axis=0).astype(BF16)
        lse_x = jnp.zeros((span, DIL_GW), F32)
        for h in range(nh):
            lse_x = jnp.where(head_of_lane == h, lse[h * span:(h + 1) * span], lse_x)
        _split_store(o_scr, (g,), cur, _dot(p_cat, v_st))
        _split_store(lse_scr, (g,), cur, lse_x)

    def run(blocks):
        staged = [scores(r, n) for r, n in blocks]
        for args in staged:
            outputs(*args)

    step = min(DIL_BLOCKS_PER_STEP, nb * dilation)
    if dilation == 1:
        def body(i, carry):
            run([(0, i * step + j) for j in range(step)])
            return carry
        lax.fori_loop(0, nb // step, body, 0)
    else:
        blocks = [(r, n) for r in range(dilation) for n in range(nb)]
        for i in range(0, len(blocks), step):
            run(blocks[i:i + step])


def _dil_kernel(cs_ref, sn_ref, q_ref, k_ref, v_ref, o_ref, s1_scr, s2_scr, qs, ks, vs, o_scr, lse_scr):
    g = pl.program_id(1)
    half = DIL_ROT_DIM // 2

    @pl.when(g == 0)
    def _():
        lane = _iota2((1, LANES), 1) % DIL_HEAD_DIM
        sn = sn_ref[0]
        s1_scr[...] = jnp.where(lane < half, -sn, 0.0)
        s2_scr[...] = jnp.where((lane >= half) & (lane < DIL_ROT_DIM), sn, 0.0)

    def both(a):
        return jnp.concatenate([a, a], axis=1)

    def rope(z):
        return (z * both(cs_ref[0]) + pltpu.roll(z, DIL_GW - half, 1) * both(s1_scr[...])
                + pltpu.roll(z, half, 1) * both(s2_scr[...]))

    everything = slice(None)
    _split_store(qs, (), everything, rope(q_ref[0].astype(F32)) * (DIL_HEAD_DIM ** -0.5))
    _split_store(ks, (), everything, rope(k_ref[0].astype(F32)))
    _split_store(vs, (), everything, v_ref[0].astype(F32))

    for gi, (window, dilation) in enumerate(DIL_GROUPS):
        assert window // dilation == DIL_SPAN

        @pl.when(g == gi)
        def _(gi=gi, dilation=dilation):
            _dil_group(gi, dilation, qs, ks, vs, o_scr, lse_scr)

    @pl.when(g == len(DIL_GROUPS) - 1)
    def _():
        for half_i in range(DIL_GW // LANES):
            lses = [lse_scr[i, half_i] for i in range(len(DIL_GROUPS))]
            m = functools.reduce(jnp.maximum, lses)
            es = [jnp.exp(l - m) for l in lses]
            den = functools.reduce(lambda a, b: a + b, es)
            num = functools.reduce(lambda a, b: a + b,
                                   [e * o_scr[i, half_i] for i, e in enumerate(es)])
            o_ref[0, :, half_i * LANES:(half_i + 1) * LANES] = (num / den).astype(BF16)


def _dilated(proj3, dil_cs, dil_sn):
    b, t, _ = proj3.shape
    ng = len(DIL_GROUPS)
    table = pltpu.VMEM((t, LANES), F32)
    split = pltpu.VMEM((DIL_GW // LANES, t, LANES), F32)
    return pl.pallas_call(
        _dil_kernel,
        grid=(b, ng),
        in_specs=[
            pl.BlockSpec((1, t, LANES), lambda i, g: (i, 0, 0)),
            pl.BlockSpec((1, t, LANES), lambda i, g: (i, 0, 0)),
            pl.BlockSpec((1, t, DIL_GW), lambda i, g: (i, 0, COL_DQ // DIL_GW + g)),
            pl.BlockSpec((1, t, DIL_GW), lambda i, g: (i, 0, COL_DK // DIL_GW + g)),
            pl.BlockSpec((1, t, DIL_GW), lambda i, g: (i, 0, COL_DV // DIL_GW + g)),
        ],
        out_specs=pl.BlockSpec((1, t, DIL_GW), lambda i, g: (i, 0, 0)),
        out_shape=jax.ShapeDtypeStruct((b, t, DIL_GW), BF16),
        scratch_shapes=[table, table, split, split, split,
                        pltpu.VMEM((ng, DIL_GW // LANES, t, LANES), F32),
                        pltpu.VMEM((ng, DIL_GW // LANES, t, LANES), F32)],
        compiler_params=_params("parallel", "arbitrary"),
        name="dilated",
    )(dil_cs, dil_sn, proj3, proj3, proj3)


def _merge_kernel(x_ref, gates_ref, oret_ref, ogdn_ref, odil_ref, wr_ref, wg_ref, wd_ref, wo_ref,
                  out_ref):
    d = x_ref.shape[1]
    branches = (_dot(oret_ref[...], wr_ref[...]), _dot(ogdn_ref[...], wg_ref[...]),
                _dot(odil_ref[...], wd_ref[...]))
    merged = None
    for bi, br in enumerate(branches):
        term = jax.nn.sigmoid(gates_ref[:, bi * d:(bi + 1) * d].astype(F32)) * br
        merged = term if merged is None else merged + term
    out_ref[...] = x_ref[...] + _dot(merged.astype(BF16), wo_ref[...])


def _merge(x2, proj2, o_ret, o_gdn, o_dil, w_ret, w_gdn, w_dil, w_out, *, tm):
    m, d = x2.shape
    full = lambda a: pl.BlockSpec(a.shape, lambda i: (0, 0))
    return pl.pallas_call(
        _merge_kernel,
        grid=(m // tm,),
        in_specs=[
            pl.BlockSpec((tm, d), lambda i: (i, 0)),
            pl.BlockSpec((tm, N_BRANCH * d), lambda i: (i, COL_GATES // (N_BRANCH * d))),
            pl.BlockSpec((tm, RET_V_W), lambda i: (i, 0)),
            pl.BlockSpec((tm, GDN_W), lambda i: (i, 0)),
            pl.BlockSpec((tm, DIL_GW), lambda i: (i, 0)),
            full(w_ret), full(w_gdn), full(w_dil), full(w_out),
        ],
        out_specs=pl.BlockSpec((tm, d), lambda i: (i, 0)),
        out_shape=jax.ShapeDtypeStruct((m, d), F32),
        compiler_params=_params("parallel"),
        name="merge",
    )(x2, proj2, o_ret, o_gdn, o_dil, w_ret, w_gdn, w_dil, w_out)


def _xattn_kernel(x_ref, g_ref, kv_ref, wq_ref, wo_ref, out_ref):
    x = x_ref[0]
    hb = _rms(x, g_ref[...]).astype(BF16)
    q = _dot(hb, wq_ref[...])
    outs = []
    for h in range(XATTN_HEADS):
        cols = slice(h * XATTN_HEAD_DIM, (h + 1) * XATTN_HEAD_DIM)
        kh = kv_ref[0, :, cols]
        vh = kv_ref[0, :, XATTN_W + h * XATTN_HEAD_DIM:XATTN_W + (h + 1) * XATTN_HEAD_DIM]
        s = _dot_nt(q[:, cols].astype(BF16), kh) * (XATTN_HEAD_DIM ** -0.5)
        e = jnp.exp(s - jnp.max(s, axis=-1, keepdims=True))
        p = e / jnp.sum(e, axis=-1, keepdims=True)
        outs.append(_dot(p.astype(BF16), vh))
    o = jnp.concatenate(outs, axis=1).astype(BF16)
    out_ref[0] = x + _dot(o, wo_ref[...])


def _xattn(x3, g, kv3, wq, wo, *, tm):
    b, t, d = x3.shape
    mem = kv3.shape[1]
    return pl.pallas_call(
        _xattn_kernel,
        grid=(b, t // tm),
        in_specs=[
            pl.BlockSpec((1, tm, d), lambda i, j: (i, j, 0)),
            pl.BlockSpec((1, d), lambda i, j: (0, 0)),
            pl.BlockSpec((1, mem, 2 * XATTN_W), lambda i, j: (i, 0, 0)),
            pl.BlockSpec(wq.shape, lambda i, j: (0, 0)),
            pl.BlockSpec(wo.shape, lambda i, j: (0, 0)),
        ],
        out_specs=pl.BlockSpec((1, tm, d), lambda i, j: (i, j, 0)),
        out_shape=jax.ShapeDtypeStruct((b, t, d), F32),
        compiler_params=_params("parallel", "parallel"),
        name="xattn",
    )(x3, g, kv3, wq, wo)


def _ffn_kernel(x_ref, g_ref, wup_ref, cw_ref, cb_ref, wd_ref, fg_ref, out_ref,
                h_scr, act_scr, a_scr, u_scr, hist_a, hist_u, *, tf, final_norm):
    tm = x_ref.shape[1]
    ffn_dim = wd_ref.shape[0]
    halo = SUBLANES

    @pl.when(pl.program_id(1) == 0)
    def _():
        hist_a[...] = jnp.zeros_like(hist_a)
        hist_u[...] = jnp.zeros_like(hist_u)

    x = x_ref[0]
    h_scr[...] = _rms(x, g_ref[...]).astype(BF16)

    def up_conv(scr, hist, c, col0):
        cols = slice(col0, col0 + tf)
        up = _dot(h_scr[...], wup_ref[:, cols])
        scr[0:halo, :] = hist[c]
        scr[halo:, :] = up
        hist[c] = up[tm - halo:tm]
        acc = cb_ref[:, cols]
        for j in range(FFN_CONV):
            off = halo - (FFN_CONV - 1) + j
            acc = acc + cw_ref[j:j + 1, cols] * scr[off:off + tm, :]
        return acc

    for c in range(ffn_dim // tf):
        a = up_conv(a_scr, hist_a, c, c * tf)
        u = up_conv(u_scr, hist_u, c, ffn_dim + c * tf)
        act_scr[:, c * tf:(c + 1) * tf] = (_silu(a) * u).astype(BF16)

    out = x + _dot(act_scr[...], wd_ref[...])
    if final_norm:
        out = _rms(out, fg_ref[...])
    out_ref[0] = out


def _ffn(x3, g, w_up, conv_w, conv_b, w_down, final_g, *, tm, tf, final_norm):
    b, t, d = x3.shape
    ffn_dim = w_down.shape[0]
    nf = ffn_dim // tf
    resident = lambda a: pl.BlockSpec(a.shape, lambda i, j: (0, 0), pipeline_mode=pl.Buffered(1))
    return pl.pallas_call(
        functools.partial(_ffn_kernel, tf=tf, final_norm=final_norm),
        grid=(b, t // tm),
        in_specs=[
            pl.BlockSpec((1, tm, d), lambda i, j: (i, j, 0)),
            resident(g), resident(w_up), resident(conv_w), resident(conv_b), resident(w_down),
            resident(final_g),
        ],
        out_specs=pl.BlockSpec((1, tm, d), lambda i, j: (i, j, 0)),
        out_shape=jax.ShapeDtypeStruct((b, t, d), F32),
        scratch_shapes=[
            pltpu.VMEM((tm, d), BF16),
            pltpu.VMEM((tm, ffn_dim), BF16),
            pltpu.VMEM((tm + SUBLANES, tf), F32),
            pltpu.VMEM((tm + SUBLANES, tf), F32),
            pltpu.VMEM((nf, SUBLANES, tf), F32),
            pltpu.VMEM((nf, SUBLANES, tf), F32),
        ],
        compiler_params=_params("parallel", "arbitrary"),
        name="ffn",
    )(x3, g, w_up, conv_w, conv_b, w_down, final_g)


def _pad_lanes(v, width):
    return jnp.pad(v.astype(F32), (0, width - v.shape[0]))[None, :]


def kernel(x, mem, positions, norm_mix_g, w_in, ret_norm_g, gdn_conv_w, gdn_a_log, gdn_dt_bias,
           gdn_norm_g, w_br_ret, w_br_gdn, w_br_dil, w_out, norm_xattn_g, norm_mem_g, xattn_wq,
           xattn_wkv, xattn_wo, norm_ffn_g, ffn_w_up, ffn_conv_w, ffn_conv_b, ffn_w_down,
           final_norm_g):
    b, t, d = x.shape
    depth = w_in.shape[0]
    mem_tokens = mem.shape[1]
    ab0 = 2 * RET_QK_W + 2 * RET_V_W + 3 * GDN_W
    ab1 = ab0 + 2 * GDN_HEADS

    pos3 = positions.astype(F32)[:, :, None]
    ret_inv = 1.0 / (RET_ROT_BASE ** jnp.linspace(0.0, 1.0, RET_QK_DIM // 2, dtype=F32))
    ret_inv_row = jnp.tile(ret_inv, LANES // ret_inv.shape[0])[None, :]
    dil_inv = ROPE_THETA ** (-jnp.arange(0, DIL_ROT_DIM, 2, dtype=F32) / DIL_ROT_DIM)
    dil_head = jnp.concatenate([dil_inv, dil_inv, jnp.zeros((DIL_HEAD_DIM - DIL_ROT_DIM,), F32)])
    dil_inv_row = jnp.tile(dil_head, LANES // DIL_HEAD_DIM)[None, :]
    ret_cs, ret_sn, dil_cs, dil_sn = _rope_tables(pos3, ret_inv_row, dil_inv_row)

    mem2 = mem.reshape(b * mem_tokens, d)
    for l in range(depth):
        w_main = _regroup_w_in(w_in, l, ab0, ab1, tk=128)
        ab_cols = lax.slice(w_in[l], (0, ab0), (d, ab0 + LANES))[:, :ab1 - ab0]
        w_ab = jnp.pad(ab_cols, ((0, 0), (0, GAB_PAD - (ab1 - ab0)))).astype(BF16)
        w_abt = ab_cols.T.astype(BF16)
        alog_l = _pad_lanes(gdn_a_log[l], GAB_PAD)
        dt_l = _pad_lanes(gdn_dt_bias[l], GAB_PAD)
        alog_r = jnp.repeat(gdn_a_log[l].astype(F32), GDN_CHUNK)[None, :]
        dt_r = jnp.repeat(gdn_dt_bias[l].astype(F32), GDN_CHUNK)[None, :]

        x2 = x.reshape(b * t, d)
        proj, gab, gabt = _in_proj(x2, norm_mix_g[l][None, :], w_main, w_ab, w_abt, tm=512, tn=1280)
        proj3 = proj.reshape(b, t, PROJ_W)
        o_ret = _retention(proj3, ret_cs, ret_sn, ret_norm_g[l])
        n_chunks = b * t // GDN_CHUNK
        gabt_rows = (gabt[:GDN_HEADS].reshape(GDN_HEADS, n_chunks, GDN_CHUNK).transpose(1, 0, 2)
                     .reshape(n_chunks, 1, GDN_HEADS * GDN_CHUNK))
        o_gdn = _gdn(proj3, gab, gabt_rows, gdn_conv_w[l], alog_l, dt_l, alog_r, dt_r,
                     gdn_norm_g[l][None, :])
        o_dil = _dilated(proj3, dil_cs, dil_sn)
        x2 = _merge(x2, proj, o_ret.reshape(b * t, RET_V_W), o_gdn.reshape(b * t, GDN_W),
                    o_dil.reshape(b * t, DIL_GW), w_br_ret[l].astype(BF16),
                    w_br_gdn[l].astype(BF16), w_br_dil[l].astype(BF16), w_out[l].astype(BF16),
                    tm=512)

        kv = _norm_matmul(mem2, norm_mem_g[l][None, :], xattn_wkv[l].astype(BF16),
                          tm=mem_tokens)
        x3 = _xattn(x2.reshape(b, t, d), norm_xattn_g[l][None, :],
                    kv.reshape(b, mem_tokens, 2 * XATTN_W), xattn_wq[l].astype(BF16),
                    xattn_wo[l].astype(BF16), tm=512)

        x = _ffn(x3, norm_ffn_g[l][None, :], ffn_w_up[l].astype(BF16), ffn_conv_w[l],
                 ffn_conv_b[l][None, :], ffn_w_down[l].astype(BF16), final_norm_g[None, :],
                 tm=512, tf=256, final_norm=(l == depth - 1))
    return x
```

```python
import functools
import math

import jax
import jax.numpy as jnp
import numpy as np
from jax import lax
from jax.experimental import pallas as pl
from jax.experimental.pallas import tpu as pltpu

F32 = jnp.float32
BF16 = jnp.bfloat16

EPS = 1e-6
NEG_INF = -1e30

RET_HEADS, RET_QK_DIM, RET_V_DIM, RET_CHUNK = 4, 64, 128, 128
RET_ROT_BASE = 10000.0
GDN_HEADS, GDN_HEAD_DIM, GDN_CONV, GDN_CHUNK = 4, 128, 4, 64
DIL_GROUPS = ((128, 1), (512, 4), (2048, 16))
DIL_HEADS, DIL_HEAD_DIM = 4, 64
DIL_ROT_DIM = DIL_HEAD_DIM // 4
DIL_SPAN = 128
ROPE_THETA = 500000.0
XATTN_HEADS, XATTN_HEAD_DIM = 4, 128
FFN_CONV = 3
N_BRANCH = 3

RET_QK_W = RET_HEADS * RET_QK_DIM
RET_V_W = RET_HEADS * RET_V_DIM
GDN_W = GDN_HEADS * GDN_HEAD_DIM
DIL_GW = DIL_HEADS * DIL_HEAD_DIM
DIL_W = len(DIL_GROUPS) * DIL_GW
XATTN_W = XATTN_HEADS * XATTN_HEAD_DIM

LANES = 128
SUBLANES = 8
VMEM_LIMIT_BYTES = 56 * 1024 * 1024

COL_GATES = 0
COL_RQ = COL_GATES + N_BRANCH * 1024
COL_RK = COL_RQ + RET_QK_W
COL_RV = COL_RK + RET_QK_W
COL_RG = COL_RV + RET_V_W
COL_GQKV = COL_RG + RET_V_W
COL_GZ = COL_GQKV + 3 * GDN_W
COL_DQ = COL_GZ + GDN_W
COL_DK = COL_DQ + DIL_W
COL_DV = COL_DK + DIL_W
PROJ_W = COL_DV + DIL_W
GAB_PAD = LANES


def _params(*sem):
    return pltpu.CompilerParams(dimension_semantics=sem, vmem_limit_bytes=VMEM_LIMIT_BYTES)


def _rms(x, g):
    return x * lax.rsqrt(jnp.mean(x * x, axis=-1, keepdims=True) + EPS) * g


def _dot(a, b):
    return jnp.dot(a, b, preferred_element_type=F32)


def _dot_nt(a, b):
    return lax.dot_general(a, b, (((1,), (1,)), ((), ())), preferred_element_type=F32)


def _dot_tn(a, b):
    return lax.dot_general(a, b, (((0,), (0,)), ((), ())), preferred_element_type=F32)


def _silu(x):
    return x * jax.nn.sigmoid(x)


def _lane_sum_bcast(x):
    hi = x.astype(BF16)
    lo = (x - hi.astype(F32)).astype(BF16)
    ones = jnp.ones((2 * LANES, LANES), BF16)
    return _dot(jnp.concatenate([hi, lo], axis=1), ones)


def _iota2(shape, dim):
    return lax.broadcasted_iota(jnp.int32, shape, dim)


def _regroup_w_in_kernel(w_ref, o_ref, wab_ref, wabt_ref, *, ab0, ab1):
    n_in = w_ref.shape[2]
    gates0 = n_in - (COL_RQ - COL_GATES)
    o_ref[:, COL_GATES:COL_RQ] = w_ref[0, :, gates0:n_in].astype(BF16)
    o_ref[:, COL_RQ:COL_RQ + ab0] = w_ref[0, :, 0:ab0].astype(BF16)
    o_ref[:, COL_RQ + ab0:PROJ_W] = w_ref[0, :, ab1:gates0].astype(BF16)
    ab = w_ref[0, :, ab0:ab0 + LANES]
    ab = jnp.where(_iota2((1, LANES), 1) < ab1 - ab0, ab, 0.0)
    wab_ref[...] = ab.astype(BF16)
    wabt_ref[...] = ab.T[0:SUBLANES, :].astype(BF16)


def _regroup_w_in(w_in, layer, ab0, ab1, *, tk):
    _, d, n_in = w_in.shape
    assert ab0 % LANES == 0 and ab1 - ab0 <= SUBLANES
    return pl.pallas_call(
        functools.partial(_regroup_w_in_kernel, ab0=ab0, ab1=ab1),
        grid=(d // tk,),
        in_specs=[pl.BlockSpec((1, tk, n_in), lambda i: (layer, i, 0))],
        out_specs=[pl.BlockSpec((tk, PROJ_W), lambda i: (i, 0)),
                   pl.BlockSpec((tk, GAB_PAD), lambda i: (i, 0)),
                   pl.BlockSpec((SUBLANES, tk), lambda i: (0, i))],
        out_shape=[jax.ShapeDtypeStruct((d, PROJ_W), BF16),
                   jax.ShapeDtypeStruct((d, GAB_PAD), BF16),
                   jax.ShapeDtypeStruct((SUBLANES, d), BF16)],
        compiler_params=_params("parallel"),
        name="regroup_w_in",
    )(w_in)


def _in_proj_kernel(x_ref, g_ref, w_ref, wab_ref, wabt_ref, proj_ref, gab_ref, gabt_ref, h_scr, *, tn):
    h_scr[...] = _rms(x_ref[...], g_ref[...]).astype(BF16)
    for j in range(w_ref.shape[1] // tn):
        cols = slice(j * tn, (j + 1) * tn)
        proj_ref[:, cols] = _dot(h_scr[...], w_ref[:, cols]).astype(BF16)
    gab_ref[...] = _dot(h_scr[...], wab_ref[...])
    gabt_ref[...] = _dot_nt(wabt_ref[...], h_scr[...])


def _in_proj(x2, g, w_main, w_ab, w_abt, *, tm, tn):
    m, d = x2.shape
    n = w_main.shape[1]
    resident = lambda a: pl.BlockSpec(a.shape, lambda i: (0, 0), pipeline_mode=pl.Buffered(1))
    return pl.pallas_call(
        functools.partial(_in_proj_kernel, tn=tn),
        grid=(m // tm,),
        in_specs=[
            pl.BlockSpec((tm, d), lambda i: (i, 0)),
            resident(g), resident(w_main), resident(w_ab), resident(w_abt),
        ],
        out_specs=[
            pl.BlockSpec((tm, n), lambda i: (i, 0)),
            pl.BlockSpec((tm, GAB_PAD), lambda i: (i, 0)),
            pl.BlockSpec((SUBLANES, tm), lambda i: (0, i)),
        ],
        out_shape=[
            jax.ShapeDtypeStruct((m, n), BF16),
            jax.ShapeDtypeStruct((m, GAB_PAD), F32),
            jax.ShapeDtypeStruct((SUBLANES, m), F32),
        ],
        scratch_shapes=[pltpu.VMEM((tm, d), BF16)],
        compiler_params=_params("parallel"),
        name="in_proj",
    )(x2, g, w_main, w_ab, w_abt)


def _norm_matmul_kernel(x_ref, g_ref, w_ref, o_ref):
    hb = _rms(x_ref[...], g_ref[...]).astype(BF16)
    o_ref[...] = _dot(hb, w_ref[...]).astype(BF16)


def _norm_matmul(x2, g, w, *, tm):
    m, d = x2.shape
    n = w.shape[1]
    return pl.pallas_call(
        _norm_matmul_kernel,
        grid=(m // tm,),
        in_specs=[
            pl.BlockSpec((tm, d), lambda i: (i, 0)),
            pl.BlockSpec((1, d), lambda i: (0, 0)),
            pl.BlockSpec((d, n), lambda i: (0, 0)),
        ],
        out_specs=pl.BlockSpec((tm, n), lambda i: (i, 0)),
        out_shape=jax.ShapeDtypeStruct((m, n), BF16),
        compiler_params=_params("parallel"),
        name="norm_matmul",
    )(x2, g, w)


def _ret_kernel(cs_ref, sn_ref, q_ref, k_ref, v_ref, gate_ref, ng_ref, o_ref, state):
    t = q_ref.shape[1]
    c = RET_CHUNK
    nh = RET_HEADS
    state[...] = jnp.zeros_like(state)

    lane = _iota2((1, RET_QK_W), 1)
    first_half = (lane % RET_QK_DIM) < (RET_QK_DIM // 2)
    head_of_lane = lane // RET_QK_DIM
    diff = (_iota2((c, c), 0) - _iota2((c, c), 1)).astype(F32)
    idx_col = _iota2((c, 1), 0).astype(F32)
    idx_row = _iota2((1, c), 1).astype(F32)
    log_gamma = [math.log1p(-(2.0 ** (-5.0 - h))) for h in range(nh)]
    decay_all = jnp.concatenate(
        [jnp.where(diff >= 0, jnp.exp(jnp.maximum(diff, 0.0) * lg), 0.0) for lg in log_gamma], axis=0)
    q_decay_all = jnp.concatenate([jnp.exp((idx_col + 1.0) * lg) for lg in log_gamma], axis=0)
    k_decay = [jnp.exp((c - 1.0 - idx_row) * lg) for lg in log_gamma]
    chunk_decay = [math.exp(c * lg) for lg in log_gamma]
    ng_all = jnp.concatenate(
        [jnp.broadcast_to(ng_ref[h:h + 1, :], (c, RET_V_DIM)) for h in range(nh)], axis=0)

    def rope(z, cs, sn):
        partner = jnp.where(first_half, -pltpu.roll(z, RET_QK_W - RET_QK_DIM // 2, 1),
                            pltpu.roll(z, RET_QK_DIM // 2, 1))
        return z * cs + partner * sn

    def chunk(ci, carry):
        r0 = pl.multiple_of(ci * c, c)
        rows = pl.ds(r0, c)
        cs, sn = cs_ref[0, rows, :], sn_ref[0, rows, :]
        cs = jnp.concatenate([cs, cs], axis=1)
        sn = jnp.concatenate([sn, sn], axis=1)
        q = rope(q_ref[0, rows, :].astype(F32), cs, sn)
        k = rope(k_ref[0, rows, :].astype(F32), cs, sn) * (RET_QK_DIM ** -0.5)
        k_t = k.T
        k_t_bf = k_t.astype(BF16)
        q_st = jnp.concatenate([jnp.where(head_of_lane == h, q, 0.0) for h in range(nh)], axis=0)
        s_bf = (_dot(q_st.astype(BF16), k_t_bf) * decay_all).astype(BF16)
        qd_bf = (q_st * q_decay_all).astype(BF16)
        vs = [v_ref[0, rows, h * RET_V_DIM:(h + 1) * RET_V_DIM] for h in range(nh)]
        sts = [state[h] for h in range(nh)]
        intra = [_dot(s_bf[h * c:(h + 1) * c], vs[h]) for h in range(nh)]
        inter = [_dot(qd_bf[h * c:(h + 1) * c], sts[h].astype(BF16)) for h in range(nh)]
        upd = [_dot((k_t * k_decay[h]).astype(BF16), vs[h]) for h in range(nh)]
        for h in range(nh):
            state[h] = sts[h] * chunk_decay[h] + upd[h]
        o_all = jnp.concatenate([intra[h] + inter[h] for h in range(nh)], axis=0)
        ss = _lane_sum_bcast(o_all * o_all)
        y_all = o_all * lax.rsqrt(ss * (1.0 / RET_V_DIM) + EPS) * ng_all
        for h in range(nh):
            cols = slice(h * RET_V_DIM, (h + 1) * RET_V_DIM)
            y = y_all[h * c:(h + 1) * c] * _silu(gate_ref[0, rows, cols].astype(F32))
            o_ref[0, rows, cols] = y.astype(BF16)
        return carry

    lax.fori_loop(0, t // c, chunk, 0)


def _retention(proj3, ret_cs, ret_sn, norm_g):
    b, t, _ = proj3.shape
    return pl.pallas_call(
        _ret_kernel,
        grid=(b,),
        in_specs=[
            pl.BlockSpec((1, t, LANES), lambda i: (i, 0, 0)),
            pl.BlockSpec((1, t, LANES), lambda i: (i, 0, 0)),
            pl.BlockSpec((1, t, RET_QK_W), lambda i: (i, 0, COL_RQ // RET_QK_W)),
            pl.BlockSpec((1, t, RET_QK_W), lambda i: (i, 0, COL_RK // RET_QK_W)),
            pl.BlockSpec((1, t, RET_V_W), lambda i: (i, 0, COL_RV // RET_V_W)),
            pl.BlockSpec((1, t, RET_V_W), lambda i: (i, 0, COL_RG // RET_V_W)),
            pl.BlockSpec((RET_HEADS, RET_V_DIM), lambda i: (0, 0)),
        ],
        out_specs=pl.BlockSpec((1, t, RET_V_W), lambda i: (i, 0, 0)),
        out_shape=jax.ShapeDtypeStruct((b, t, RET_V_W), BF16),
        scratch_shapes=[pltpu.VMEM((RET_HEADS, RET_QK_W, RET_V_DIM), F32)],
        compiler_params=_params("parallel"),
        name="retention",
    )(ret_cs, ret_sn, proj3, proj3, proj3, proj3, norm_g)


GDN_CHUNKS_PER_STEP = 4
GDN_CHUNKS_PER_BLOCK = 8
GDN_ROWS_PER_STEP = 4


def _softplus(x):
    return jnp.maximum(x, 0.0) + jnp.log1p(jnp.exp(-jnp.abs(x)))


def _gdn_prep_kernel(qkv_ref, gab_ref, gabt_ref, cw_ref, alog_l_ref, dt_l_ref, alog_r_ref, dt_r_ref,
                     u_ref, w_ref, qg_ref, kg_ref, attn_ref, egl_ref):
    c = GDN_CHUNK
    hd = GDN_HEAD_DIM
    nh = GDN_HEADS
    m = nh * c
    win = 2 * c
    n_shifted = GDN_CONV - 1
    block0 = pl.program_id(1) * GDN_CHUNKS_PER_BLOCK
    sel_row, sel_col = _iota2((n_shifted * c, win), 0), _iota2((n_shifted * c, win), 1)
    sel_target = (sel_row % c) + (sel_row // c) - n_shifted

    row, col = _iota2((m, m), 0), _iota2((m, m), 1)
    same_head = (row // c) == (col // c)
    tri_bd = same_head & (row >= col)
    strict_bd = same_head & (row > col)
    triu_bd = (same_head & (row <= col)).astype(F32)
    tril_c = (_iota2((c, c), 0) >= _iota2((c, c), 1)).astype(F32)
    scale = hd ** -0.5
    n_doublings = int(math.log2(c)) - 1

    def stack_heads(y, base):
        return jnp.concatenate([y[:, base + h * hd:base + (h + 1) * hd] for h in range(nh)], axis=0)

    def stack_cols(a, first):
        return jnp.concatenate([a[:, first + h:first + h + 1] for h in range(nh)], axis=0)

    def prepare(j):
        ci = block0 + j
        r0 = pl.multiple_of(ci * c, c)
        rows = pl.ds(r0, c)
        start = pl.multiple_of(jnp.maximum(r0 - c, 0), c)
        shift = jnp.where(ci > 0, c, 0)
        sel = (sel_col == sel_target + shift).astype(BF16)
        shifted = _dot(sel, qkv_ref[0, pl.ds(start, win), :])
        y = cw_ref[n_shifted:GDN_CONV, :] * qkv_ref[0, rows, :].astype(F32)
        for tap in range(n_shifted):
            y = y + cw_ref[tap:tap + 1, :] * shifted[tap * c:(tap + 1) * c]
        y = _silu(y)
        q, k, v = stack_heads(y, 0), stack_heads(y, GDN_W), stack_heads(y, 2 * GDN_W)

        gab = gab_ref[rows, :]
        g_col = -jnp.exp(alog_l_ref[...]) * _softplus(gab + dt_l_ref[...])
        gc_col = jnp.dot(tril_c, g_col, precision=lax.Precision.HIGHEST,
                         preferred_element_type=F32)
        beta = stack_cols(jax.nn.sigmoid(gab), nh)
        gcc = stack_cols(gc_col, 0)
        g_last = [gc_col[c - 1:c, h:h + 1] for h in range(nh)]
        g_last_all = jnp.concatenate([jnp.broadcast_to(g, (c, 1)) for g in g_last], axis=0)
        g_row = -jnp.exp(alog_r_ref[...]) * _softplus(gabt_ref[ci] + dt_r_ref[...])
        gcr = jnp.dot(jnp.broadcast_to(g_row, (SUBLANES, m)), triu_bd,
                      precision=lax.Precision.HIGHEST, preferred_element_type=F32)[0:1, :]
        decay = jnp.where(tri_bd, jnp.exp(jnp.where(tri_bd, gcc - gcr, 0.0)), 0.0)

        q = q * lax.rsqrt(_lane_sum_bcast(q * q) + EPS) * scale
        k = k * lax.rsqrt(_lane_sum_bcast(k * k) + EPS)
        k_t_bf = k.T.astype(BF16)
        kb = k * beta
        eg = jnp.exp(gcc)
        raw = _dot(jnp.concatenate([kb, q], axis=0).astype(BF16), k_t_bf)
        n_mat = jnp.where(strict_bd, raw[0:m] * decay, 0.0)
        attn_ref[0, j] = jnp.where(tri_bd, raw[m:2 * m] * decay, 0.0).astype(BF16)
        qg_ref[0, j] = (q * eg).astype(BF16)
        kg_ref[0, j] = (k * jnp.exp(g_last_all - gcc)).astype(BF16)
        egl_ref[0, j] = jnp.concatenate(
            [jnp.broadcast_to(jnp.exp(g), (1, hd)) for g in g_last]
            + [jnp.zeros((SUBLANES - nh, hd), F32)], axis=0)
        return dict(n_pow=n_mat, x=jnp.concatenate([v * beta, kb * eg], axis=1))

    def step(si, carry):
        first = si * GDN_CHUNKS_PER_STEP
        chunks = [prepare(first + j) for j in range(GDN_CHUNKS_PER_STEP)]
        for d in chunks:
            d["x"] = d["x"] - _dot(d["n_pow"].astype(BF16), d["x"].astype(BF16))
        for _ in range(n_doublings):
            for d in chunks:
                p_bf = d["n_pow"].astype(BF16)
                d["n_pow"] = _dot(p_bf, p_bf)
            for d in chunks:
                d["x"] = d["x"] + _dot(d["n_pow"].astype(BF16), d["x"].astype(BF16))
        for j, d in enumerate(chunks):
            u_ref[0, first + j] = d["x"][:, :hd]
            w_ref[0, first + j] = d["x"][:, hd:].astype(BF16)
        return carry

    lax.fori_loop(0, GDN_CHUNKS_PER_BLOCK // GDN_CHUNKS_PER_STEP, step, 0)


def _gdn_rec_kernel(u_ref, w_ref, qg_ref, kg_ref, attn_ref, egl_ref, z_ref, ng_ref, o_ref, state):
    c = GDN_CHUNK
    hd = GDN_HEAD_DIM
    nh = GDN_HEADS
    nr = u_ref.shape[0]
    hrows = [slice(h * c, (h + 1) * c) for h in range(nh)]
    ng_all = jnp.broadcast_to(ng_ref[...], (nh * c, hd))

    @pl.when(pl.program_id(1) == 0)
    def _():
        state[...] = jnp.zeros_like(state)

    def chunk(j, carry):
        rows = pl.ds(pl.multiple_of(j * c, c), c)
        sts = [[state[r, h] for h in range(nh)] for r in range(nr)]
        sts_bf = [[s.astype(BF16) for s in row] for row in sts]
        w_bf = [w_ref[r, j] for r in range(nr)]
        qg_bf = [qg_ref[r, j] for r in range(nr)]
        v_new = [[u_ref[r, j, hrows[h], :] - _dot(w_bf[r][hrows[h]], sts_bf[r][h]) for h in range(nh)]
                 for r in range(nr)]
        o_state = [[_dot(qg_bf[r][hrows[h]], sts_bf[r][h]) for h in range(nh)] for r in range(nr)]
        v_new_bf = [[vn.astype(BF16) for vn in row] for row in v_new]
        for r in range(nr):
            for h in range(nh):
                state[r, h] = (sts[r][h] * egl_ref[r, j, h:h + 1, :]
                               + _dot_tn(kg_ref[r, j, hrows[h], :], v_new_bf[r][h]))
        for r in range(nr):
            o = (jnp.concatenate(o_state[r], axis=0)
                 + _dot(attn_ref[r, j], jnp.concatenate(v_new_bf[r], axis=0)))
            y = o * lax.rsqrt(_lane_sum_bcast(o * o) * (1.0 / hd) + EPS) * ng_all
            for h in range(nh):
                cols = slice(h * hd, (h + 1) * hd)
                o_ref[r, rows, cols] = (y[hrows[h]] * _silu(z_ref[r, rows, cols].astype(F32))).astype(BF16)
        return carry

    lax.fori_loop(0, u_ref.shape[1], chunk, 0)


def _gdn(proj3, gab, gabt_rows, conv_w, alog_l, dt_l, alog_r, dt_r, norm_g):
    b, t, _ = proj3.shape
    c, hd, nh = GDN_CHUNK, GDN_HEAD_DIM, GDN_HEADS
    m = nh * c
    nc = t // c
    cb = GDN_CHUNKS_PER_BLOCK
    per_chunk = lambda width, dtype: jax.ShapeDtypeStruct((b, nc, m, width), dtype)
    chunk_spec = lambda width: pl.BlockSpec((1, cb, m, width), lambda i, j: (i, j, 0, 0))
    u, w, qg, kg, attn, egl = pl.pallas_call(
        _gdn_prep_kernel,
        grid=(b, nc // cb),
        in_specs=[
            pl.BlockSpec((1, t, 3 * GDN_W), lambda i, j: (i, 0, COL_GQKV // (3 * GDN_W))),
            pl.BlockSpec((t, GAB_PAD), lambda i, j: (i, 0)),
            pl.BlockSpec((nc, 1, m), lambda i, j: (i, 0, 0)),
            pl.BlockSpec((GDN_CONV, 3 * GDN_W), lambda i, j: (0, 0)),
            pl.BlockSpec((1, GAB_PAD), lambda i, j: (0, 0)),
            pl.BlockSpec((1, GAB_PAD), lambda i, j: (0, 0)),
            pl.BlockSpec((1, m), lambda i, j: (0, 0)),
            pl.BlockSpec((1, m), lambda i, j: (0, 0)),
        ],
        out_specs=[chunk_spec(hd), chunk_spec(hd), chunk_spec(hd), chunk_spec(hd), chunk_spec(m),
                   pl.BlockSpec((1, cb, SUBLANES, hd), lambda i, j: (i, j, 0, 0))],
        out_shape=[per_chunk(hd, F32), per_chunk(hd, BF16), per_chunk(hd, BF16), per_chunk(hd, BF16),
                   per_chunk(m, BF16), jax.ShapeDtypeStruct((b, nc, SUBLANES, hd), F32)],
        compiler_params=_params("parallel", "arbitrary"),
        name="gdn_prep",
    )(proj3, gab, gabt_rows, conv_w, alog_l, dt_l, alog_r, dt_r)

    nr = GDN_ROWS_PER_STEP
    rec_spec = lambda width: pl.BlockSpec((nr, cb, m, width), lambda i, j: (i, j, 0, 0))
    return pl.pallas_call(
        _gdn_rec_kernel,
        grid=(b // nr, nc // cb),
        in_specs=[rec_spec(hd), rec_spec(hd), rec_spec(hd), rec_spec(hd), rec_spec(m),
                  pl.BlockSpec((nr, cb, SUBLANES, hd), lambda i, j: (i, j, 0, 0)),
                  pl.BlockSpec((nr, cb * c, GDN_W), lambda i, j: (i, j, COL_GZ // GDN_W)),
                  pl.BlockSpec((1, hd), lambda i, j: (0, 0))],
        out_specs=pl.BlockSpec((nr, cb * c, GDN_W), lambda i, j: (i, j, 0)),
        out_shape=jax.ShapeDtypeStruct((b, t, GDN_W), BF16),
        scratch_shapes=[pltpu.VMEM((nr, nh, hd, hd), F32)],
        compiler_params=_params("parallel", "arbitrary"),
        name="gdn_rec",
    )(u, w, qg, kg, attn, egl, proj3, norm_g)


def _rope_tables_kernel(pos_ref, ret_inv_ref, dil_inv_ref, ret_cs_ref, ret_sn_ref, dil_cs_ref,
                        dil_sn_ref):
    pos = pos_ref[0]
    ang = pos * ret_inv_ref[...]
    ret_cs_ref[0] = jnp.cos(ang)
    ret_sn_ref[0] = jnp.sin(ang)
    ang = pos * dil_inv_ref[...]
    dil_cs_ref[0] = jnp.cos(ang)
    dil_sn_ref[0] = jnp.sin(ang)


def _rope_tables(pos3, ret_inv_row, dil_inv_row):
    b, t, _ = pos3.shape
    table = jax.ShapeDtypeStruct((b, t, LANES), F32)
    spec = pl.BlockSpec((1, t, LANES), lambda i: (i, 0, 0))
    return pl.pallas_call(
        _rope_tables_kernel,
        grid=(b,),
        in_specs=[
            pl.BlockSpec((1, t, 1), lambda i: (i, 0, 0)),
            pl.BlockSpec((1, LANES), lambda i: (0, 0)),
            pl.BlockSpec((1, LANES), lambda i: (0, 0)),
        ],
        out_specs=[spec, spec, spec, spec],
        out_shape=[table, table, table, table],
        compiler_params=_params("parallel"),
        name="rope_tables",
    )(pos3, ret_inv_row, dil_inv_row)


DIL_BLOCKS_PER_STEP = 4


def _split_store(ref, lead, rows, val):
    for half in range(val.shape[1] // LANES):
        ref[lead + (half, rows, slice(None))] = val[:, half * LANES:(half + 1) * LANES]


def _split_load(ref, lead, rows):
    return jnp.concatenate([ref[lead + (half, rows, slice(None))] for half in range(DIL_GW // LANES)],
                           axis=1)


def _dil_group(g, dilation, qs, ks, vs, o_scr, lse_scr):
    t = qs.shape[1]
    span = DIL_SPAN
    nh = DIL_HEADS
    nb = t // dilation // span
    nk = 2 * span if nb > 1 else span
    head_of_lane = _iota2((1, DIL_GW), 1) // DIL_HEAD_DIM
    iq = _iota2((nh * span, nk), 0) % span
    ik = _iota2((nh * span, nk), 1)
    if nb > 1:
        dist = span + iq - ik
        band = (dist >= 0) & (dist <= span)
    else:
        band = iq >= ik

    def rows_of(r, n):
        if dilation == 1:
            return pl.ds(pl.multiple_of(span * n, span), span)
        return pl.ds(r + dilation * span * n, span, stride=dilation)

    def scores(r, n):
        cur = rows_of(r, n)
        qb = _split_load(qs, (), cur)
        if nb > 1:
            prev = rows_of(r, jnp.maximum(n - 1, 0) if dilation == 1 else max(n - 1, 0))
            kk = jnp.concatenate([_split_load(ks, (), prev), _split_load(ks, (), cur)], axis=0)
            vv = jnp.concatenate([_split_load(vs, (), prev), _split_load(vs, (), cur)], axis=0)
            valid = band & (ik >= jnp.where(n > 0, 0, span))
        else:
            kk, vv, valid = _split_load(ks, (), cur), _split_load(vs, (), cur), band
        q_st = jnp.concatenate([jnp.where(head_of_lane == h, qb, 0.0) for h in range(nh)], axis=0)
        s = _dot_nt(q_st.astype(BF16), kk.astype(BF16))
        s = jnp.where(valid, s, NEG_INF)
        m = jnp.max(s, axis=-1, keepdims=True)
        p = jnp.exp(s - m)
        den = jnp.sum(p, axis=-1, keepdims=True)
        pn = (p * (1.0 / den)).astype(BF16)
        return cur, pn, vv, m + jnp.log(den)

    def outputs(cur, pn, vv, lse):
        p_cat = jnp.concatenate([pn[h * span:(h + 1) * span] for h in range(nh)], axis=1)
        v_st = jnp.concatenate([jnp.where(head_of_lane == h, vv, 0.0) for h in range(nh)],
                               axis=0).astype(BF16)
        lse_x = jnp.zeros((span, DIL_GW), F32)
        for h in range(nh):
            lse_x = jnp.where(head_of_lane == h, lse[h * span:(h + 1) * span], lse_x)
        _split_store(o_scr, (g,), cur, _dot(p_cat, v_st))
        _split_store(lse_scr, (g,), cur, lse_x)

    def run(blocks):
        staged = [scores(r, n) for r, n in blocks]
        for args in staged:
            outputs(*args)

    step = min(DIL_BLOCKS_PER_STEP, nb * dilation)
    if dilation == 1:
        def body(i, carry):
            run([(0, i * step + j) for j in range(step)])
            return carry
        lax.fori_loop(0, nb // step, body, 0)
    else:
        blocks = [(r, n) for r in range(dilation) for n in range(nb)]
        for i in range(0, len(blocks), step):
            run(blocks[i:i + step])


def _dil_kernel(cs_ref, sn_ref, q_ref, k_ref, v_ref, o_ref, s1_scr, s2_scr, qs, ks, vs, o_scr, lse_scr):
    g = pl.program_id(1)
    half = DIL_ROT_DIM // 2

    @pl.when(g == 0)
    def _():
        lane = _iota2((1, LANES), 1) % DIL_HEAD_DIM
        sn = sn_ref[0]
        s1_scr[...] = jnp.where(lane < half, -sn, 0.0)
        s2_scr[...] = jnp.where((lane >= half) & (lane < DIL_ROT_DIM), sn, 0.0)

    def both(a):
        return jnp.concatenate([a, a], axis=1)

    def rope(z):
        return (z * both(cs_ref[0]) + pltpu.roll(z, DIL_GW - half, 1) * both(s1_scr[...])
                + pltpu.roll(z, half, 1) * both(s2_scr[...]))

    everything = slice(None)
    _split_store(qs, (), everything, rope(q_ref[0].astype(F32)) * (DIL_HEAD_DIM ** -0.5))
    _split_store(ks, (), everything, rope(k_ref[0].astype(F32)))
    _split_store(vs, (), everything, v_ref[0].astype(F32))

    for gi, (window, dilation) in enumerate(DIL_GROUPS):
        assert window // dilation == DIL_SPAN

        @pl.when(g == gi)
        def _(gi=gi, dilation=dilation):
            _dil_group(gi, dilation, qs, ks, vs, o_scr, lse_scr)

    @pl.when(g == len(DIL_GROUPS) - 1)
    def _():
        for half_i in range(DIL_GW // LANES):
            lses = [lse_scr[i, half_i] for i in range(len(DIL_GROUPS))]
            m = functools.reduce(jnp.maximum, lses)
            es = [jnp.exp(l - m) for l in lses]
            den = functools.reduce(lambda a, b: a + b, es)
            num = functools.reduce(lambda a, b: a + b,
                                   [e * o_scr[i, half_i] for i, e in enumerate(es)])
            o_ref[0, :, half_i * LANES:(half_i + 1) * LANES] = (num / den).astype(BF16)


def _dilated(proj3, dil_cs, dil_sn):
    b, t, _ = proj3.shape
    ng = len(DIL_GROUPS)
    table = pltpu.VMEM((t, LANES), F32)
    split = pltpu.VMEM((DIL_GW // LANES, t, LANES), F32)
    return pl.pallas_call(
        _dil_kernel,
        grid=(b, ng),
        in_specs=[
            pl.BlockSpec((1, t, LANES), lambda i, g: (i, 0, 0)),
            pl.BlockSpec((1, t, LANES), lambda i, g: (i, 0, 0)),
            pl.BlockSpec((1, t, DIL_GW), lambda i, g: (i, 0, COL_DQ // DIL_GW + g)),
            pl.BlockSpec((1, t, DIL_GW), lambda i, g: (i, 0, COL_DK // DIL_GW + g)),
            pl.BlockSpec((1, t, DIL_GW), lambda i, g: (i, 0, COL_DV // DIL_GW + g)),
        ],
        out_specs=pl.BlockSpec((1, t, DIL_GW), lambda i, g: (i, 0, 0)),
        out_shape=jax.ShapeDtypeStruct((b, t, DIL_GW), BF16),
        scratch_shapes=[table, table, split, split, split,
                        pltpu.VMEM((ng, DIL_GW // LANES, t, LANES), F32),
                        pltpu.VMEM((ng, DIL_GW // LANES, t, LANES), F32)],
        compiler_params=_params("parallel", "arbitrary"),
        name="dilated",
    )(dil_cs, dil_sn, proj3, proj3, proj3)


def _merge_kernel(x_ref, gates_ref, oret_ref, ogdn_ref, odil_ref, wr_ref, wg_ref, wd_ref, wo_ref,
                  out_ref):
    d = x_ref.shape[1]
    branches = (_dot(oret_ref[...], wr_ref[...]), _dot(ogdn_ref[...], wg_ref[...]),
                _dot(odil_ref[...], wd_ref[...]))
    merged = None
    for bi, br in enumerate(branches):
        term = jax.nn.sigmoid(gates_ref[:, bi * d:(bi + 1) * d].astype(F32)) * br
        merged = term if merged is None else merged + term
    out_ref[...] = x_ref[...] + _dot(merged.astype(BF16), wo_ref[...])


def _merge(x2, proj2, o_ret, o_gdn, o_dil, w_ret, w_gdn, w_dil, w_out, *, tm):
    m, d = x2.shape
    full = lambda a: pl.BlockSpec(a.shape, lambda i: (0, 0))
    return pl.pallas_call(
        _merge_kernel,
        grid=(m // tm,),
        in_specs=[
            pl.BlockSpec((tm, d), lambda i: (i, 0)),
            pl.BlockSpec((tm, N_BRANCH * d), lambda i: (i, COL_GATES // (N_BRANCH * d))),
            pl.BlockSpec((tm, RET_V_W), lambda i: (i, 0)),
            pl.BlockSpec((tm, GDN_W), lambda i: (i, 0)),
            pl.BlockSpec((tm, DIL_GW), lambda i: (i, 0)),
            full(w_ret), full(w_gdn), full(w_dil), full(w_out),
        ],
        out_specs=pl.BlockSpec((tm, d), lambda i: (i, 0)),
        out_shape=jax.ShapeDtypeStruct((m, d), F32),
        compiler_params=_params("parallel"),
        name="merge",
    )(x2, proj2, o_ret, o_gdn, o_dil, w_ret, w_gdn, w_dil, w_out)


def _xattn_kernel(x_ref, g_ref, kv_ref, wq_ref, wo_ref, out_ref):
    x = x_ref[0]
    hb = _rms(x, g_ref[...]).astype(BF16)
    q = _dot(hb, wq_ref[...])
    outs = []
    for h in range(XATTN_HEADS):
        cols = slice(h * XATTN_HEAD_DIM, (h + 1) * XATTN_HEAD_DIM)
        kh = kv_ref[0, :, cols]
        vh = kv_ref[0, :, XATTN_W + h * XATTN_HEAD_DIM:XATTN_W + (h + 1) * XATTN_HEAD_DIM]
        s = _dot_nt(q[:, cols].astype(BF16), kh) * (XATTN_HEAD_DIM ** -0.5)
        e = jnp.exp(s - jnp.max(s, axis=-1, keepdims=True))
        p = e / jnp.sum(e, axis=-1, keepdims=True)
        outs.append(_dot(p.astype(BF16), vh))
    o = jnp.concatenate(outs, axis=1).astype(BF16)
    out_ref[0] = x + _dot(o, wo_ref[...])


def _xattn(x3, g, kv3, wq, wo, *, tm):
    b, t, d = x3.shape
    mem = kv3.shape[1]
    return pl.pallas_call(
        _xattn_kernel,
        grid=(b, t // tm),
        in_specs=[
            pl.BlockSpec((1, tm, d), lambda i, j: (i, j, 0)),
            pl.BlockSpec((1, d), lambda i, j: (0, 0)),
            pl.BlockSpec((1, mem, 2 * XATTN_W), lambda i, j: (i, 0, 0)),
            pl.BlockSpec(wq.shape, lambda i, j: (0, 0)),
            pl.BlockSpec(wo.shape, lambda i, j: (0, 0)),
        ],
        out_specs=pl.BlockSpec((1, tm, d), lambda i, j: (i, j, 0)),
        out_shape=jax.ShapeDtypeStruct((b, t, d), F32),
        compiler_params=_params("parallel", "parallel"),
        name="xattn",
    )(x3, g, kv3, wq, wo)


def _ffn_kernel(x_ref, g_ref, wup_ref, cw_ref, cb_ref, wd_ref, fg_ref, out_ref,
                h_scr, act_scr, a_scr, u_scr, hist_a, hist_u, *, tf, final_norm):
    tm = x_ref.shape[1]
    ffn_dim = wd_ref.shape[0]
    halo = SUBLANES

    @pl.when(pl.program_id(1) == 0)
    def _():
        hist_a[...] = jnp.zeros_like(hist_a)
        hist_u[...] = jnp.zeros_like(hist_u)

    x = x_ref[0]
    h_scr[...] = _rms(x, g_ref[...]).astype(BF16)

    def up_proj(scr, hist, c, col0):
        up = _dot(h_scr[...], wup_ref[:, col0:col0 + tf])
        scr[c % 2, 0:halo, :] = hist[c]
        scr[c % 2, halo:, :] = up
        hist[c] = up[tm - halo:tm]

    def conv(scr, c, col0):
        cols = slice(col0, col0 + tf)
        acc = cb_ref[:, cols]
        for j in range(FFN_CONV):
            off = halo - (FFN_CONV - 1) + j
            acc = acc + cw_ref[j:j + 1, cols] * scr[c % 2, off:off + tm, :]
        return acc

    nf = ffn_dim // tf
    up_proj(a_scr, hist_a, 0, 0)
    up_proj(u_scr, hist_u, 0, ffn_dim)
    for c in range(nf):
        if c + 1 < nf:
            up_proj(a_scr, hist_a, c + 1, (c + 1) * tf)
            up_proj(u_scr, hist_u, c + 1, ffn_dim + (c + 1) * tf)
        act = _silu(conv(a_scr, c, c * tf)) * conv(u_scr, c, ffn_dim + c * tf)
        act_scr[:, c * tf:(c + 1) * tf] = act.astype(BF16)

    out = x + _dot(act_scr[...], wd_ref[...])
    if final_norm:
        out = _rms(out, fg_ref[...])
    out_ref[0] = out


def _ffn(x3, g, w_up, conv_w, conv_b, w_down, final_g, *, tm, tf, final_norm):
    b, t, d = x3.shape
    ffn_dim = w_down.shape[0]
    nf = ffn_dim // tf
    resident = lambda a: pl.BlockSpec(a.shape, lambda i, j: (0, 0), pipeline_mode=pl.Buffered(1))
    return pl.pallas_call(
        functools.partial(_ffn_kernel, tf=tf, final_norm=final_norm),
        grid=(b, t // tm),
        in_specs=[
            pl.BlockSpec((1, tm, d), lambda i, j: (i, j, 0)),
            resident(g), resident(w_up), resident(conv_w), resident(conv_b), resident(w_down),
            resident(final_g),
        ],
        out_specs=pl.BlockSpec((1, tm, d), lambda i, j: (i, j, 0)),
        out_shape=jax.ShapeDtypeStruct((b, t, d), F32),
        scratch_shapes=[
            pltpu.VMEM((tm, d), BF16),
            pltpu.VMEM((tm, ffn_dim), BF16),
            pltpu.VMEM((2, tm + SUBLANES, tf), F32),
            pltpu.VMEM((2, tm + SUBLANES, tf), F32),
            pltpu.VMEM((nf, SUBLANES, tf), F32),
            pltpu.VMEM((nf, SUBLANES, tf), F32),
        ],
        compiler_params=_params("parallel", "arbitrary"),
        name="ffn",
    )(x3, g, w_up, conv_w, conv_b, w_down, final_g)


def _pad_lanes(v, width):
    return jnp.pad(v.astype(F32), (0, width - v.shape[0]))[None, :]


def kernel(x, mem, positions, norm_mix_g, w_in, ret_norm_g, gdn_conv_w, gdn_a_log, gdn_dt_bias,
           gdn_norm_g, w_br_ret, w_br_gdn, w_br_dil, w_out, norm_xattn_g, norm_mem_g, xattn_wq,
           xattn_wkv, xattn_wo, norm_ffn_g, ffn_w_up, ffn_conv_w, ffn_conv_b, ffn_w_down,
           final_norm_g):
    b, t, d = x.shape
    depth = w_in.shape[0]
    mem_tokens = mem.shape[1]
    ab0 = 2 * RET_QK_W + 2 * RET_V_W + 3 * GDN_W
    ab1 = ab0 + 2 * GDN_HEADS

    pos3 = positions.astype(F32)[:, :, None]
    ret_inv = 1.0 / (RET_ROT_BASE ** jnp.linspace(0.0, 1.0, RET_QK_DIM // 2, dtype=F32))
    ret_inv_row = jnp.tile(ret_inv, LANES // ret_inv.shape[0])[None, :]
    dil_inv = ROPE_THETA ** (-jnp.arange(0, DIL_ROT_DIM, 2, dtype=F32) / DIL_ROT_DIM)
    dil_head = jnp.concatenate([dil_inv, dil_inv, jnp.zeros((DIL_HEAD_DIM - DIL_ROT_DIM,), F32)])
    dil_inv_row = jnp.tile(dil_head, LANES // DIL_HEAD_DIM)[None, :]
    ret_cs, ret_sn, dil_cs, dil_sn = _rope_tables(pos3, ret_inv_row, dil_inv_row)

    mem2 = mem.reshape(b * mem_tokens, d)
    for l in range(depth):
        w_main, w_ab, w_abt = _regroup_w_in(w_in, l, ab0, ab1, tk=128)
        alog_l = _pad_lanes(gdn_a_log[l], GAB_PAD)
        dt_l = _pad_lanes(gdn_dt_bias[l], GAB_PAD)
        alog_r = jnp.repeat(gdn_a_log[l].astype(F32), GDN_CHUNK)[None, :]
        dt_r = jnp.repeat(gdn_dt_bias[l].astype(F32), GDN_CHUNK)[None, :]

        x2 = x.reshape(b * t, d)
        proj, gab, gabt = _in_proj(x2, norm_mix_g[l][None, :], w_main, w_ab, w_abt, tm=512, tn=1280)
        proj3 = proj.reshape(b, t, PROJ_W)
        o_ret = _retention(proj3, ret_cs, ret_sn, ret_norm_g[l])
        n_chunks = b * t // GDN_CHUNK
        gabt_rows = (gabt[:GDN_HEADS].reshape(GDN_HEADS, n_chunks, GDN_CHUNK).transpose(1, 0, 2)
                     .reshape(n_chunks, 1, GDN_HEADS * GDN_CHUNK))
        o_gdn = _gdn(proj3, gab, gabt_rows, gdn_conv_w[l], alog_l, dt_l, alog_r, dt_r,
                     gdn_norm_g[l][None, :])
        o_dil = _dilated(proj3, dil_cs, dil_sn)
        x2 = _merge(x2, proj, o_ret.reshape(b * t, RET_V_W), o_gdn.reshape(b * t, GDN_W),
                    o_dil.reshape(b * t, DIL_GW), w_br_ret[l].astype(BF16),
                    w_br_gdn[l].astype(BF16), w_br_dil[l].astype(BF16), w_out[l].astype(BF16),
                    tm=512)

        kv = _norm_matmul(mem2, norm_mem_g[l][None, :], xattn_wkv[l].astype(BF16),
                          tm=mem_tokens)
        x3 = _xattn(x2.reshape(b, t, d), norm_xattn_g[l][None, :],
                    kv.reshape(b, mem_tokens, 2 * XATTN_W), xattn_wq[l].astype(BF16),
                    xattn_wo[l].astype(BF16), tm=512)

        x = _ffn(x3, norm_ffn_g[l][None, :], ffn_w_up[l].astype(BF16), ffn_conv_w[l],
                 ffn_conv_b[l][None, :], ffn_w_down[l].astype(BF16), final_norm_g[None, :],
                 tm=512, tf=1408, final_norm=(l == depth - 1))
    return x
```

```python
import functools
import math

import jax
import jax.numpy as jnp
import numpy as np
from jax import lax
from jax.experimental import pallas as pl
from jax.experimental.pallas import tpu as pltpu

F32 = jnp.float32
BF16 = jnp.bfloat16

EPS = 1e-6
NEG_INF = -1e30

RET_HEADS, RET_QK_DIM, RET_V_DIM, RET_CHUNK = 4, 64, 128, 128
RET_ROT_BASE = 10000.0
GDN_HEADS, GDN_HEAD_DIM, GDN_CONV, GDN_CHUNK = 4, 128, 4, 64
DIL_GROUPS = ((128, 1), (512, 4), (2048, 16))
DIL_HEADS, DIL_HEAD_DIM = 4, 64
DIL_ROT_DIM = DIL_HEAD_DIM // 4
DIL_SPAN = 128
ROPE_THETA = 500000.0
XATTN_HEADS, XATTN_HEAD_DIM = 4, 128
FFN_CONV = 3
N_BRANCH = 3

RET_QK_W = RET_HEADS * RET_QK_DIM
RET_V_W = RET_HEADS * RET_V_DIM
GDN_W = GDN_HEADS * GDN_HEAD_DIM
DIL_GW = DIL_HEADS * DIL_HEAD_DIM
DIL_W = len(DIL_GROUPS) * DIL_GW
XATTN_W = XATTN_HEADS * XATTN_HEAD_DIM

LANES = 128
SUBLANES = 8
VMEM_LIMIT_BYTES = 56 * 1024 * 1024

COL_GATES = 0
COL_RQ = COL_GATES + N_BRANCH * 1024
COL_RK = COL_RQ + RET_QK_W
COL_RV = COL_RK + RET_QK_W
COL_RG = COL_RV + RET_V_W
COL_GQKV = COL_RG + RET_V_W
COL_GZ = COL_GQKV + 3 * GDN_W
COL_DQ = COL_GZ + GDN_W
COL_DK = COL_DQ + DIL_W
COL_DV = COL_DK + DIL_W
PROJ_W = COL_DV + DIL_W
GAB_PAD = LANES


def _params(*sem):
    return pltpu.CompilerParams(dimension_semantics=sem, vmem_limit_bytes=VMEM_LIMIT_BYTES)


def _rms(x, g):
    return x * lax.rsqrt(jnp.mean(x * x, axis=-1, keepdims=True) + EPS) * g


def _dot(a, b):
    return jnp.dot(a, b, preferred_element_type=F32)


def _dot_nt(a, b):
    return lax.dot_general(a, b, (((1,), (1,)), ((), ())), preferred_element_type=F32)


def _dot_tn(a, b):
    return lax.dot_general(a, b, (((0,), (0,)), ((), ())), preferred_element_type=F32)


def _silu(x):
    return x * jax.nn.sigmoid(x)


def _lane_sum_bcast(x):
    hi = x.astype(BF16)
    lo = (x - hi.astype(F32)).astype(BF16)
    ones = jnp.ones((2 * LANES, LANES), BF16)
    return _dot(jnp.concatenate([hi, lo], axis=1), ones)


def _iota2(shape, dim):
    return lax.broadcasted_iota(jnp.int32, shape, dim)


def _regroup_w_in_kernel(w_ref, o_ref, wab_ref, wabt_ref, *, ab0, ab1):
    n_in = w_ref.shape[2]
    gates0 = n_in - (COL_RQ - COL_GATES)
    o_ref[:, COL_GATES:COL_RQ] = w_ref[0, :, gates0:n_in].astype(BF16)
    o_ref[:, COL_RQ:COL_RQ + ab0] = w_ref[0, :, 0:ab0].astype(BF16)
    o_ref[:, COL_RQ + ab0:PROJ_W] = w_ref[0, :, ab1:gates0].astype(BF16)
    ab = w_ref[0, :, ab0:ab0 + LANES]
    ab = jnp.where(_iota2((1, LANES), 1) < ab1 - ab0, ab, 0.0)
    wab_ref[...] = ab.astype(BF16)
    wabt_ref[...] = ab.T[0:SUBLANES, :].astype(BF16)


def _regroup_w_in(w_in, layer, ab0, ab1, *, tk):
    _, d, n_in = w_in.shape
    assert ab0 % LANES == 0 and ab1 - ab0 <= SUBLANES
    return pl.pallas_call(
        functools.partial(_regroup_w_in_kernel, ab0=ab0, ab1=ab1),
        grid=(d // tk,),
        in_specs=[pl.BlockSpec((1, tk, n_in), lambda i: (layer, i, 0))],
        out_specs=[pl.BlockSpec((tk, PROJ_W), lambda i: (i, 0)),
                   pl.BlockSpec((tk, GAB_PAD), lambda i: (i, 0)),
                   pl.BlockSpec((SUBLANES, tk), lambda i: (0, i))],
        out_shape=[jax.ShapeDtypeStruct((d, PROJ_W), BF16),
                   jax.ShapeDtypeStruct((d, GAB_PAD), BF16),
                   jax.ShapeDtypeStruct((SUBLANES, d), BF16)],
        compiler_params=_params("parallel"),
        name="regroup_w_in",
    )(w_in)


def _in_proj_kernel(x_ref, g_ref, w_ref, wab_ref, wabt_ref, proj_ref, gab_ref, gabt_ref, h_scr, *, tn):
    h_scr[...] = _rms(x_ref[...], g_ref[...]).astype(BF16)
    for j in range(w_ref.shape[1] // tn):
        cols = slice(j * tn, (j + 1) * tn)
        proj_ref[:, cols] = _dot(h_scr[...], w_ref[:, cols]).astype(BF16)
    gab_ref[...] = _dot(h_scr[...], wab_ref[...])
    gabt_ref[...] = _dot_nt(wabt_ref[...], h_scr[...])


def _in_proj(x2, g, w_main, w_ab, w_abt, *, tm, tn):
    m, d = x2.shape
    n = w_main.shape[1]
    resident = lambda a: pl.BlockSpec(a.shape, lambda i: (0, 0), pipeline_mode=pl.Buffered(1))
    return pl.pallas_call(
        functools.partial(_in_proj_kernel, tn=tn),
        grid=(m // tm,),
        in_specs=[
            pl.BlockSpec((tm, d), lambda i: (i, 0)),
            resident(g), resident(w_main), resident(w_ab), resident(w_abt),
        ],
        out_specs=[
            pl.BlockSpec((tm, n), lambda i: (i, 0)),
            pl.BlockSpec((tm, GAB_PAD), lambda i: (i, 0)),
            pl.BlockSpec((SUBLANES, tm), lambda i: (0, i)),
        ],
        out_shape=[
            jax.ShapeDtypeStruct((m, n), BF16),
            jax.ShapeDtypeStruct((m, GAB_PAD), F32),
            jax.ShapeDtypeStruct((SUBLANES, m), F32),
        ],
        scratch_shapes=[pltpu.VMEM((tm, d), BF16)],
        compiler_params=_params("parallel"),
        name="in_proj",
    )(x2, g, w_main, w_ab, w_abt)


def _norm_matmul_kernel(x_ref, g_ref, w_ref, o_ref):
    hb = _rms(x_ref[...], g_ref[...]).astype(BF16)
    o_ref[...] = _dot(hb, w_ref[...]).astype(BF16)


def _norm_matmul(x2, g, w, *, tm):
    m, d = x2.shape
    n = w.shape[1]
    return pl.pallas_call(
        _norm_matmul_kernel,
        grid=(m // tm,),
        in_specs=[
            pl.BlockSpec((tm, d), lambda i: (i, 0)),
            pl.BlockSpec((1, d), lambda i: (0, 0)),
            pl.BlockSpec((d, n), lambda i: (0, 0)),
        ],
        out_specs=pl.BlockSpec((tm, n), lambda i: (i, 0)),
        out_shape=jax.ShapeDtypeStruct((m, n), BF16),
        compiler_params=_params("parallel"),
        name="norm_matmul",
    )(x2, g, w)


def _ret_kernel(cs_ref, sn_ref, q_ref, k_ref, v_ref, gate_ref, ng_ref, o_ref, state):
    t = q_ref.shape[1]
    c = RET_CHUNK
    nh = RET_HEADS
    state[...] = jnp.zeros_like(state)

    lane = _iota2((1, RET_QK_W), 1)
    first_half = (lane % RET_QK_DIM) < (RET_QK_DIM // 2)
    head_of_lane = lane // RET_QK_DIM
    diff = (_iota2((c, c), 0) - _iota2((c, c), 1)).astype(F32)
    idx_col = _iota2((c, 1), 0).astype(F32)
    idx_row = _iota2((1, c), 1).astype(F32)
    log_gamma = [math.log1p(-(2.0 ** (-5.0 - h))) for h in range(nh)]
    decay_all = jnp.concatenate(
        [jnp.where(diff >= 0, jnp.exp(jnp.maximum(diff, 0.0) * lg), 0.0) for lg in log_gamma], axis=0)
    q_decay_all = jnp.concatenate([jnp.exp((idx_col + 1.0) * lg) for lg in log_gamma], axis=0)
    k_decay = [jnp.exp((c - 1.0 - idx_row) * lg) for lg in log_gamma]
    chunk_decay = [math.exp(c * lg) for lg in log_gamma]
    ng_all = jnp.concatenate(
        [jnp.broadcast_to(ng_ref[h:h + 1, :], (c, RET_V_DIM)) for h in range(nh)], axis=0)

    def rope(z, cs, sn):
        partner = jnp.where(first_half, -pltpu.roll(z, RET_QK_W - RET_QK_DIM // 2, 1),
                            pltpu.roll(z, RET_QK_DIM // 2, 1))
        return z * cs + partner * sn

    batch_rows = range(q_ref.shape[0])
    heads = range(nh)
    hrows = [slice(h * c, (h + 1) * c) for h in heads]
    hcols = [slice(h * RET_V_DIM, (h + 1) * RET_V_DIM) for h in heads]

    def chunk(ci, carry):
        r0 = pl.multiple_of(ci * c, c)
        rows = pl.ds(r0, c)
        q_st, k_t = [], []
        for r in batch_rows:
            cs, sn = cs_ref[r, rows, :], sn_ref[r, rows, :]
            cs = jnp.concatenate([cs, cs], axis=1)
            sn = jnp.concatenate([sn, sn], axis=1)
            q = rope(q_ref[r, rows, :].astype(F32), cs, sn)
            k = rope(k_ref[r, rows, :].astype(F32), cs, sn) * (RET_QK_DIM ** -0.5)
            k_t.append(k.T)
            q_st.append(jnp.concatenate([jnp.where(head_of_lane == h, q, 0.0) for h in heads], axis=0))
        s_bf = [(_dot(q_st[r].astype(BF16), k_t[r].astype(BF16)) * decay_all).astype(BF16)
                for r in batch_rows]
        qd_bf = [(q_st[r] * q_decay_all).astype(BF16) for r in batch_rows]
        vs = [[v_ref[r, rows, hcols[h]] for h in heads] for r in batch_rows]
        sts = [[state[r, h] for h in heads] for r in batch_rows]
        intra = [[_dot(s_bf[r][hrows[h]], vs[r][h]) for h in heads] for r in batch_rows]
        inter = [[_dot(qd_bf[r][hrows[h]], sts[r][h].astype(BF16)) for h in heads] for r in batch_rows]
        upd = [[_dot((k_t[r] * k_decay[h]).astype(BF16), vs[r][h]) for h in heads] for r in batch_rows]
        for r in batch_rows:
            for h in heads:
                state[r, h] = sts[r][h] * chunk_decay[h] + upd[r][h]
        for r in batch_rows:
            o_all = jnp.concatenate([intra[r][h] + inter[r][h] for h in heads], axis=0)
            ss = _lane_sum_bcast(o_all * o_all)
            y_all = o_all * lax.rsqrt(ss * (1.0 / RET_V_DIM) + EPS) * ng_all
            for h in heads:
                y = y_all[hrows[h]] * _silu(gate_ref[r, rows, hcols[h]].astype(F32))
                o_ref[r, rows, hcols[h]] = y.astype(BF16)
        return carry

    lax.fori_loop(0, t // c, chunk, 0)


RET_ROWS_PER_STEP = 2


def _retention(proj3, ret_cs, ret_sn, norm_g):
    b, t, _ = proj3.shape
    nr = RET_ROWS_PER_STEP
    rows = lambda width, col=0: pl.BlockSpec((nr, t, width), lambda i: (i, 0, col))
    return pl.pallas_call(
        _ret_kernel,
        grid=(b // nr,),
        in_specs=[
            rows(LANES), rows(LANES),
            rows(RET_QK_W, COL_RQ // RET_QK_W), rows(RET_QK_W, COL_RK // RET_QK_W),
            rows(RET_V_W, COL_RV // RET_V_W), rows(RET_V_W, COL_RG // RET_V_W),
            pl.BlockSpec((RET_HEADS, RET_V_DIM), lambda i: (0, 0)),
        ],
        out_specs=rows(RET_V_W),
        out_shape=jax.ShapeDtypeStruct((b, t, RET_V_W), BF16),
        scratch_shapes=[pltpu.VMEM((nr, RET_HEADS, RET_QK_W, RET_V_DIM), F32)],
        compiler_params=_params("parallel"),
        name="retention",
    )(ret_cs, ret_sn, proj3, proj3, proj3, proj3, norm_g)


GDN_CHUNKS_PER_STEP = 4
GDN_CHUNKS_PER_BLOCK = 8
GDN_ROWS_PER_STEP = 4


def _softplus(x):
    return jnp.maximum(x, 0.0) + jnp.log1p(jnp.exp(-jnp.abs(x)))


def _gdn_prep_kernel(qkv_ref, gab_ref, gabt_ref, cw_ref, alog_l_ref, dt_l_ref, alog_r_ref, dt_r_ref,
                     u_ref, w_ref, qg_ref, kg_ref, attn_ref, egl_ref):
    c = GDN_CHUNK
    hd = GDN_HEAD_DIM
    nh = GDN_HEADS
    m = nh * c
    win = 2 * c
    n_shifted = GDN_CONV - 1
    block0 = pl.program_id(1) * GDN_CHUNKS_PER_BLOCK
    sel_row, sel_col = _iota2((n_shifted * c, win), 0), _iota2((n_shifted * c, win), 1)
    sel_target = (sel_row % c) + (sel_row // c) - n_shifted

    row, col = _iota2((m, m), 0), _iota2((m, m), 1)
    same_head = (row // c) == (col // c)
    tri_bd = same_head & (row >= col)
    strict_bd = same_head & (row > col)
    triu_bd = (same_head & (row <= col)).astype(F32)
    tril_c = (_iota2((c, c), 0) >= _iota2((c, c), 1)).astype(F32)
    scale = hd ** -0.5
    n_doublings = int(math.log2(c)) - 1

    def stack_heads(y, base):
        return jnp.concatenate([y[:, base + h * hd:base + (h + 1) * hd] for h in range(nh)], axis=0)

    def stack_cols(a, first):
        return jnp.concatenate([a[:, first + h:first + h + 1] for h in range(nh)], axis=0)

    def prepare(j):
        ci = block0 + j
        r0 = pl.multiple_of(ci * c, c)
        rows = pl.ds(r0, c)
        start = pl.multiple_of(jnp.maximum(r0 - c, 0), c)
        shift = jnp.where(ci > 0, c, 0)
        sel = (sel_col == sel_target + shift).astype(BF16)
        shifted = _dot(sel, qkv_ref[0, pl.ds(start, win), :])
        y = cw_ref[n_shifted:GDN_CONV, :] * qkv_ref[0, rows, :].astype(F32)
        for tap in range(n_shifted):
            y = y + cw_ref[tap:tap + 1, :] * shifted[tap * c:(tap + 1) * c]
        y = _silu(y)
        q, k, v = stack_heads(y, 0), stack_heads(y, GDN_W), stack_heads(y, 2 * GDN_W)

        gab = gab_ref[rows, :]
        g_col = -jnp.exp(alog_l_ref[...]) * _softplus(gab + dt_l_ref[...])
        gc_col = jnp.dot(tril_c, g_col, precision=lax.Precision.HIGHEST,
                         preferred_element_type=F32)
        beta = stack_cols(jax.nn.sigmoid(gab), nh)
        gcc = stack_cols(gc_col, 0)
        g_last = [gc_col[c - 1:c, h:h + 1] for h in range(nh)]
        g_last_all = jnp.concatenate([jnp.broadcast_to(g, (c, 1)) for g in g_last], axis=0)
        g_row = -jnp.exp(alog_r_ref[...]) * _softplus(gabt_ref[ci] + dt_r_ref[...])
        gcr = jnp.dot(jnp.broadcast_to(g_row, (SUBLANES, m)), triu_bd,
                      precision=lax.Precision.HIGHEST, preferred_element_type=F32)[0:1, :]
        decay = jnp.where(tri_bd, jnp.exp(jnp.where(tri_bd, gcc - gcr, 0.0)), 0.0)

        q = q * lax.rsqrt(_lane_sum_bcast(q * q) + EPS) * scale
        k = k * lax.rsqrt(_lane_sum_bcast(k * k) + EPS)
        k_t_bf = k.T.astype(BF16)
        kb = k * beta
        eg = jnp.exp(gcc)
        raw = _dot(jnp.concatenate([kb, q], axis=0).astype(BF16), k_t_bf)
        n_mat = jnp.where(strict_bd, raw[0:m] * decay, 0.0)
        attn_ref[0, j] = jnp.where(tri_bd, raw[m:2 * m] * decay, 0.0).astype(BF16)
        qg_ref[0, j] = (q * eg).astype(BF16)
        kg_ref[0, j] = (k * jnp.exp(g_last_all - gcc)).astype(BF16)
        egl_ref[0, j] = jnp.concatenate(
            [jnp.broadcast_to(jnp.exp(g), (1, hd)) for g in g_last]
            + [jnp.zeros((SUBLANES - nh, hd), F32)], axis=0)
        return dict(n_pow=n_mat, x=jnp.concatenate([v * beta, kb * eg], axis=1))

    def step(si, carry):
        first = si * GDN_CHUNKS_PER_STEP
        chunks = [prepare(first + j) for j in range(GDN_CHUNKS_PER_STEP)]
        for d in chunks:
            d["x"] = d["x"] - _dot(d["n_pow"].astype(BF16), d["x"].astype(BF16))
        for _ in range(n_doublings):
            for d in chunks:
                p_bf = d["n_pow"].astype(BF16)
                d["n_pow"] = _dot(p_bf, p_bf)
            for d in chunks:
                d["x"] = d["x"] + _dot(d["n_pow"].astype(BF16), d["x"].astype(BF16))
        for j, d in enumerate(chunks):
            u_ref[0, first + j] = d["x"][:, :hd]
            w_ref[0, first + j] = d["x"][:, hd:].astype(BF16)
        return carry

    lax.fori_loop(0, GDN_CHUNKS_PER_BLOCK // GDN_CHUNKS_PER_STEP, step, 0)


def _gdn_rec_kernel(u_ref, w_ref, qg_ref, kg_ref, attn_ref, egl_ref, z_ref, ng_ref, o_ref, state):
    c = GDN_CHUNK
    hd = GDN_HEAD_DIM
    nh = GDN_HEADS
    nr = u_ref.shape[0]
    hrows = [slice(h * c, (h + 1) * c) for h in range(nh)]
    ng_all = jnp.broadcast_to(ng_ref[...], (nh * c, hd))

    @pl.when(pl.program_id(1) == 0)
    def _():
        state[...] = jnp.zeros_like(state)

    def chunk(j, carry):
        rows = pl.ds(pl.multiple_of(j * c, c), c)
        sts = [[state[r, h] for h in range(nh)] for r in range(nr)]
        sts_bf = [[s.astype(BF16) for s in row] for row in sts]
        w_bf = [w_ref[r, j] for r in range(nr)]
        qg_bf = [qg_ref[r, j] for r in range(nr)]
        v_new = [[u_ref[r, j, hrows[h], :] - _dot(w_bf[r][hrows[h]], sts_bf[r][h]) for h in range(nh)]
                 for r in range(nr)]
        o_state = [[_dot(qg_bf[r][hrows[h]], sts_bf[r][h]) for h in range(nh)] for r in range(nr)]
        v_new_bf = [[vn.astype(BF16) for vn in row] for row in v_new]
        for r in range(nr):
            for h in range(nh):
                state[r, h] = (sts[r][h] * egl_ref[r, j, h:h + 1, :]
                               + _dot_tn(kg_ref[r, j, hrows[h], :], v_new_bf[r][h]))
        for r in range(nr):
            o = (jnp.concatenate(o_state[r], axis=0)
                 + _dot(attn_ref[r, j], jnp.concatenate(v_new_bf[r], axis=0)))
            y = o * lax.rsqrt(_lane_sum_bcast(o * o) * (1.0 / hd) + EPS) * ng_all
            for h in range(nh):
                cols = slice(h * hd, (h + 1) * hd)
                o_ref[r, rows, cols] = (y[hrows[h]] * _silu(z_ref[r, rows, cols].astype(F32))).astype(BF16)
        return carry

    lax.fori_loop(0, u_ref.shape[1], chunk, 0)


def _gdn(proj3, gab, gabt_rows, conv_w, alog_l, dt_l, alog_r, dt_r, norm_g):
    b, t, _ = proj3.shape
    c, hd, nh = GDN_CHUNK, GDN_HEAD_DIM, GDN_HEADS
    m = nh * c
    nc = t // c
    cb = GDN_CHUNKS_PER_BLOCK
    per_chunk = lambda width, dtype: jax.ShapeDtypeStruct((b, nc, m, width), dtype)
    chunk_spec = lambda width: pl.BlockSpec((1, cb, m, width), lambda i, j: (i, j, 0, 0))
    u, w, qg, kg, attn, egl = pl.pallas_call(
        _gdn_prep_kernel,
        grid=(b, nc // cb),
        in_specs=[
            pl.BlockSpec((1, t, 3 * GDN_W), lambda i, j: (i, 0, COL_GQKV // (3 * GDN_W))),
            pl.BlockSpec((t, GAB_PAD), lambda i, j: (i, 0)),
            pl.BlockSpec((nc, 1, m), lambda i, j: (i, 0, 0)),
            pl.BlockSpec((GDN_CONV, 3 * GDN_W), lambda i, j: (0, 0)),
            pl.BlockSpec((1, GAB_PAD), lambda i, j: (0, 0)),
            pl.BlockSpec((1, GAB_PAD), lambda i, j: (0, 0)),
            pl.BlockSpec((1, m), lambda i, j: (0, 0)),
            pl.BlockSpec((1, m), lambda i, j: (0, 0)),
        ],
        out_specs=[chunk_spec(hd), chunk_spec(hd), chunk_spec(hd), chunk_spec(hd), chunk_spec(m),
                   pl.BlockSpec((1, cb, SUBLANES, hd), lambda i, j: (i, j, 0, 0))],
        out_shape=[per_chunk(hd, F32), per_chunk(hd, BF16), per_chunk(hd, BF16), per_chunk(hd, BF16),
                   per_chunk(m, BF16), jax.ShapeDtypeStruct((b, nc, SUBLANES, hd), F32)],
        compiler_params=_params("parallel", "arbitrary"),
        name="gdn_prep",
    )(proj3, gab, gabt_rows, conv_w, alog_l, dt_l, alog_r, dt_r)

    nr = GDN_ROWS_PER_STEP
    rec_spec = lambda width: pl.BlockSpec((nr, cb, m, width), lambda i, j: (i, j, 0, 0))
    return pl.pallas_call(
        _gdn_rec_kernel,
        grid=(b // nr, nc // cb),
        in_specs=[rec_spec(hd), rec_spec(hd), rec_spec(hd), rec_spec(hd), rec_spec(m),
                  pl.BlockSpec((nr, cb, SUBLANES, hd), lambda i, j: (i, j, 0, 0)),
                  pl.BlockSpec((nr, cb * c, GDN_W), lambda i, j: (i, j, COL_GZ // GDN_W)),
                  pl.BlockSpec((1, hd), lambda i, j: (0, 0))],
        out_specs=pl.BlockSpec((nr, cb * c, GDN_W), lambda i, j: (i, j, 0)),
        out_shape=jax.ShapeDtypeStruct((b, t, GDN_W), BF16),
        scratch_shapes=[pltpu.VMEM((nr, nh, hd, hd), F32)],
        compiler_params=_params("parallel", "arbitrary"),
        name="gdn_rec",
    )(u, w, qg, kg, attn, egl, proj3, norm_g)


def _rope_tables_kernel(pos_ref, ret_inv_ref, dil_inv_ref, ret_cs_ref, ret_sn_ref, dil_cs_ref,
                        dil_sn_ref):
    pos = pos_ref[0]
    ang = pos * ret_inv_ref[...]
    ret_cs_ref[0] = jnp.cos(ang)
    ret_sn_ref[0] = jnp.sin(ang)
    ang = pos * dil_inv_ref[...]
    dil_cs_ref[0] = jnp.cos(ang)
    dil_sn_ref[0] = jnp.sin(ang)


def _rope_tables(pos3, ret_inv_row, dil_inv_row):
    b, t, _ = pos3.shape
    table = jax.ShapeDtypeStruct((b, t, LANES), F32)
    spec = pl.BlockSpec((1, t, LANES), lambda i: (i, 0, 0))
    return pl.pallas_call(
        _rope_tables_kernel,
        grid=(b,),
        in_specs=[
            pl.BlockSpec((1, t, 1), lambda i: (i, 0, 0)),
            pl.BlockSpec((1, LANES), lambda i: (0, 0)),
            pl.BlockSpec((1, LANES), lambda i: (0, 0)),
        ],
        out_specs=[spec, spec, spec, spec],
        out_shape=[table, table, table, table],
        compiler_params=_params("parallel"),
        name="rope_tables",
    )(pos3, ret_inv_row, dil_inv_row)


DIL_BLOCKS_PER_STEP = 4


def _split_store(ref, lead, rows, val):
    for half in range(val.shape[1] // LANES):
        ref[lead + (half, rows, slice(None))] = val[:, half * LANES:(half + 1) * LANES]


def _split_load(ref, lead, rows):
    return jnp.concatenate([ref[lead + (half, rows, slice(None))] for half in range(DIL_GW // LANES)],
                           axis=1)


def _dil_group(g, dilation, qs, ks, vs, o_scr, lse_scr):
    t = qs.shape[1]
    span = DIL_SPAN
    nh = DIL_HEADS
    nb = t // dilation // span
    nk = 2 * span if nb > 1 else span
    head_of_lane = _iota2((1, DIL_GW), 1) // DIL_HEAD_DIM
    iq = _iota2((nh * span, nk), 0) % span
    ik = _iota2((nh * span, nk), 1)
    if nb > 1:
        dist = span + iq - ik
        band = (dist >= 0) & (dist <= span)
    else:
        band = iq >= ik

    def rows_of(r, n):
        if dilation == 1:
            return pl.ds(pl.multiple_of(span * n, span), span)
        return pl.ds(r + dilation * span * n, span, stride=dilation)

    def scores(r, n):
        cur = rows_of(r, n)
        qb = _split_load(qs, (), cur)
        if nb > 1:
            prev = rows_of(r, jnp.maximum(n - 1, 0) if dilation == 1 else max(n - 1, 0))
            kk = jnp.concatenate([_split_load(ks, (), prev), _split_load(ks, (), cur)], axis=0)
            vv = jnp.concatenate([_split_load(vs, (), prev), _split_load(vs, (), cur)], axis=0)
            valid = band & (ik >= jnp.where(n > 0, 0, span))
        else:
            kk, vv, valid = _split_load(ks, (), cur), _split_load(vs, (), cur), band
        q_st = jnp.concatenate([jnp.where(head_of_lane == h, qb, 0.0) for h in range(nh)], axis=0)
        s = _dot_nt(q_st.astype(BF16), kk.astype(BF16))
        s = jnp.where(valid, s, NEG_INF)
        m = jnp.max(s, axis=-1, keepdims=True)
        p = jnp.exp(s - m)
        den = jnp.sum(p, axis=-1, keepdims=True)
        pn = (p * (1.0 / den)).astype(BF16)
        return cur, pn, vv, m + jnp.log(den)

    def outputs(cur, pn, vv, lse):
        p_cat = jnp.concatenate([pn[h * span:(h + 1) * span] for h in range(nh)], axis=1)
        v_st = jnp.concatenate([jnp.where(head_of_lane == h, vv, 0.0) for h in range(nh)],
                               axis=0).astype(BF16)
        lse_x = jnp.zeros((span, DIL_GW), F32)
        for h in range(nh):
            lse_x = jnp.where(head_of_lane == h, lse[h * span:(h + 1) * span], lse_x)
        _split_store(o_scr, (g,), cur, _dot(p_cat, v_st))
        _split_store(lse_scr, (g,), cur, lse_x)

    def run(blocks):
        staged = [scores(r, n) for r, n in blocks]
        for args in staged:
            outputs(*args)

    step = min(DIL_BLOCKS_PER_STEP, nb * dilation)
    if dilation == 1:
        def body(i, carry):
            run([(0, i * step + j) for j in range(step)])
            return carry
        lax.fori_loop(0, nb // step, body, 0)
    else:
        blocks = [(r, n) for r in range(dilation) for n in range(nb)]
        for i in range(0, len(blocks), step):
            run(blocks[i:i + step])


def _dil_kernel(cs_ref, sn_ref, q_ref, k_ref, v_ref, o_ref, s1_scr, s2_scr, qs, ks, vs, o_scr, lse_scr):
    g = pl.program_id(1)
    half = DIL_ROT_DIM // 2

    @pl.when(g == 0)
    def _():
        lane = _iota2((1, LANES), 1) % DIL_HEAD_DIM
        sn = sn_ref[0]
        s1_scr[...] = jnp.where(lane < half, -sn, 0.0)
        s2_scr[...] = jnp.where((lane >= half) & (lane < DIL_ROT_DIM), sn, 0.0)

    def both(a):
        return jnp.concatenate([a, a], axis=1)

    def rope(z):
        return (z * both(cs_ref[0]) + pltpu.roll(z, DIL_GW - half, 1) * both(s1_scr[...])
                + pltpu.roll(z, half, 1) * both(s2_scr[...]))

    everything = slice(None)
    _split_store(qs, (), everything, rope(q_ref[0].astype(F32)) * (DIL_HEAD_DIM ** -0.5))
    _split_store(ks, (), everything, rope(k_ref[0].astype(F32)))
    _split_store(vs, (), everything, v_ref[0].astype(F32))

    for gi, (window, dilation) in enumerate(DIL_GROUPS):
        assert window // dilation == DIL_SPAN

        @pl.when(g == gi)
        def _(gi=gi, dilation=dilation):
            _dil_group(gi, dilation, qs, ks, vs, o_scr, lse_scr)

    @pl.when(g == len(DIL_GROUPS) - 1)
    def _():
        for half_i in range(DIL_GW // LANES):
            lses = [lse_scr[i, half_i] for i in range(len(DIL_GROUPS))]
            m = functools.reduce(jnp.maximum, lses)
            es = [jnp.exp(l - m) for l in lses]
            den = functools.reduce(lambda a, b: a + b, es)
            num = functools.reduce(lambda a, b: a + b,
                                   [e * o_scr[i, half_i] for i, e in enumerate(es)])
            o_ref[0, :, half_i * LANES:(half_i + 1) * LANES] = (num / den).astype(BF16)


def _dilated(proj3, dil_cs, dil_sn):
    b, t, _ = proj3.shape
    ng = len(DIL_GROUPS)
    table = pltpu.VMEM((t, LANES), F32)
    split = pltpu.VMEM((DIL_GW // LANES, t, LANES), F32)
    return pl.pallas_call(
        _dil_kernel,
        grid=(b, ng),
        in_specs=[
            pl.BlockSpec((1, t, LANES), lambda i, g: (i, 0, 0)),
            pl.BlockSpec((1, t, LANES), lambda i, g: (i, 0, 0)),
            pl.BlockSpec((1, t, DIL_GW), lambda i, g: (i, 0, COL_DQ // DIL_GW + g)),
            pl.BlockSpec((1, t, DIL_GW), lambda i, g: (i, 0, COL_DK // DIL_GW + g)),
            pl.BlockSpec((1, t, DIL_GW), lambda i, g: (i, 0, COL_DV // DIL_GW + g)),
        ],
        out_specs=pl.BlockSpec((1, t, DIL_GW), lambda i, g: (i, 0, 0)),
        out_shape=jax.ShapeDtypeStruct((b, t, DIL_GW), BF16),
        scratch_shapes=[table, table, split, split, split,
                        pltpu.VMEM((ng, DIL_GW // LANES, t, LANES), F32),
                        pltpu.VMEM((ng, DIL_GW // LANES, t, LANES), F32)],
        compiler_params=_params("parallel", "arbitrary"),
        name="dilated",
    )(dil_cs, dil_sn, proj3, proj3, proj3)


def _merge_xattn_kernel(x_ref, gates_ref, oret_ref, ogdn_ref, odil_ref, kv_ref, wr_ref, wg_ref,
                        wd_ref, wo_ref, g_ref, wq_ref, wxo_ref, out_ref):
    d = x_ref.shape[2]
    branches = (_dot(oret_ref[0], wr_ref[...]), _dot(ogdn_ref[0], wg_ref[...]),
                _dot(odil_ref[0], wd_ref[...]))
    merged = None
    for bi, br in enumerate(branches):
        term = jax.nn.sigmoid(gates_ref[0, :, bi * d:(bi + 1) * d].astype(F32)) * br
        merged = term if merged is None else merged + term
    y = x_ref[0] + _dot(merged.astype(BF16), wo_ref[...])

    q = _dot(_rms(y, g_ref[...]).astype(BF16), wq_ref[...])
    outs = []
    for h in range(XATTN_HEADS):
        cols = slice(h * XATTN_HEAD_DIM, (h + 1) * XATTN_HEAD_DIM)
        kh = kv_ref[0, :, cols]
        vh = kv_ref[0, :, XATTN_W + h * XATTN_HEAD_DIM:XATTN_W + (h + 1) * XATTN_HEAD_DIM]
        s = _dot_nt(q[:, cols].astype(BF16), kh) * (XATTN_HEAD_DIM ** -0.5)
        e = jnp.exp(s - jnp.max(s, axis=-1, keepdims=True))
        p = e / jnp.sum(e, axis=-1, keepdims=True)
        outs.append(_dot(p.astype(BF16), vh))
    o = jnp.concatenate(outs, axis=1).astype(BF16)
    out_ref[0] = y + _dot(o, wxo_ref[...])


def _merge_xattn(x3, proj3, o_ret, o_gdn, o_dil, kv3, w_ret, w_gdn, w_dil, w_out, g, wq, wxo, *, tm):
    b, t, d = x3.shape
    mem = kv3.shape[1]
    rows = lambda width, col=0: pl.BlockSpec((1, tm, width), lambda i, j: (i, j, col))
    resident = lambda a: pl.BlockSpec(a.shape, lambda i, j: (0, 0), pipeline_mode=pl.Buffered(1))
    return pl.pallas_call(
        _merge_xattn_kernel,
        grid=(b, t // tm),
        in_specs=[
            rows(d), rows(N_BRANCH * d, COL_GATES // (N_BRANCH * d)), rows(RET_V_W), rows(GDN_W),
            rows(DIL_GW),
            pl.BlockSpec((1, mem, 2 * XATTN_W), lambda i, j: (i, 0, 0)),
            resident(w_ret), resident(w_gdn), resident(w_dil), resident(w_out), resident(g),
            resident(wq), resident(wxo),
        ],
        out_specs=rows(d),
        out_shape=jax.ShapeDtypeStruct((b, t, d), F32),
        compiler_params=_params("parallel", "parallel"),
        name="merge_xattn",
    )(x3, proj3, o_ret, o_gdn, o_dil, kv3, w_ret, w_gdn, w_dil, w_out, g, wq, wxo)


def _ffn_kernel(x_ref, g_ref, wup_ref, cw_ref, cb_ref, wd_ref, fg_ref, out_ref,
                h_scr, act_scr, a_scr, u_scr, hist_a, hist_u, *, tf, final_norm):
    tm = x_ref.shape[1]
    ffn_dim = wd_ref.shape[0]
    halo = SUBLANES

    @pl.when(pl.program_id(1) == 0)
    def _():
        hist_a[...] = jnp.zeros_like(hist_a)
        hist_u[...] = jnp.zeros_like(hist_u)

    x = x_ref[0]
    h_scr[...] = _rms(x, g_ref[...]).astype(BF16)

    def up_proj(scr, hist, c, col0):
        up = _dot(h_scr[...], wup_ref[:, col0:col0 + tf])
        scr[c % 2, 0:halo, :] = hist[c]
        scr[c % 2, halo:, :] = up
        hist[c] = up[tm - halo:tm]

    def conv(scr, c, col0):
        cols = slice(col0, col0 + tf)
        acc = cb_ref[:, cols]
        for j in range(FFN_CONV):
            off = halo - (FFN_CONV - 1) + j
            acc = acc + cw_ref[j:j + 1, cols] * scr[c % 2, off:off + tm, :]
        return acc

    nf = ffn_dim // tf
    up_proj(a_scr, hist_a, 0, 0)
    up_proj(u_scr, hist_u, 0, ffn_dim)
    for c in range(nf):
        if c + 1 < nf:
            up_proj(a_scr, hist_a, c + 1, (c + 1) * tf)
            up_proj(u_scr, hist_u, c + 1, ffn_dim + (c + 1) * tf)
        act = _silu(conv(a_scr, c, c * tf)) * conv(u_scr, c, ffn_dim + c * tf)
        act_scr[:, c * tf:(c + 1) * tf] = act.astype(BF16)

    out = x + _dot(act_scr[...], wd_ref[...])
    if final_norm:
        out = _rms(out, fg_ref[...])
    out_ref[0] = out


def _ffn(x3, g, w_up, conv_w, conv_b, w_down, final_g, *, tm, tf, final_norm):
    b, t, d = x3.shape
    ffn_dim = w_down.shape[0]
    nf = ffn_dim // tf
    resident = lambda a: pl.BlockSpec(a.shape, lambda i, j: (0, 0), pipeline_mode=pl.Buffered(1))
    return pl.pallas_call(
        functools.partial(_ffn_kernel, tf=tf, final_norm=final_norm),
        grid=(b, t // tm),
        in_specs=[
            pl.BlockSpec((1, tm, d), lambda i, j: (i, j, 0)),
            resident(g), resident(w_up), resident(conv_w), resident(conv_b), resident(w_down),
            resident(final_g),
        ],
        out_specs=pl.BlockSpec((1, tm, d), lambda i, j: (i, j, 0)),
        out_shape=jax.ShapeDtypeStruct((b, t, d), F32),
        scratch_shapes=[
            pltpu.VMEM((tm, d), BF16),
            pltpu.VMEM((tm, ffn_dim), BF16),
            pltpu.VMEM((2, tm + SUBLANES, tf), F32),
            pltpu.VMEM((2, tm + SUBLANES, tf), F32),
            pltpu.VMEM((nf, SUBLANES, tf), F32),
            pltpu.VMEM((nf, SUBLANES, tf), F32),
        ],
        compiler_params=_params("parallel", "arbitrary"),
        name="ffn",
    )(x3, g, w_up, conv_w, conv_b, w_down, final_g)


def _pad_lanes(v, width):
    return jnp.pad(v.astype(F32), (0, width - v.shape[0]))[None, :]


def kernel(x, mem, positions, norm_mix_g, w_in, ret_norm_g, gdn_conv_w, gdn_a_log, gdn_dt_bias,
           gdn_norm_g, w_br_ret, w_br_gdn, w_br_dil, w_out, norm_xattn_g, norm_mem_g, xattn_wq,
           xattn_wkv, xattn_wo, norm_ffn_g, ffn_w_up, ffn_conv_w, ffn_conv_b, ffn_w_down,
           final_norm_g):
    b, t, d = x.shape
    depth = w_in.shape[0]
    mem_tokens = mem.shape[1]
    ab0 = 2 * RET_QK_W + 2 * RET_V_W + 3 * GDN_W
    ab1 = ab0 + 2 * GDN_HEADS

    pos3 = positions.astype(F32)[:, :, None]
    ret_inv = 1.0 / (RET_ROT_BASE ** jnp.linspace(0.0, 1.0, RET_QK_DIM // 2, dtype=F32))
    ret_inv_row = jnp.tile(ret_inv, LANES // ret_inv.shape[0])[None, :]
    dil_inv = ROPE_THETA ** (-jnp.arange(0, DIL_ROT_DIM, 2, dtype=F32) / DIL_ROT_DIM)
    dil_head = jnp.concatenate([dil_inv, dil_inv, jnp.zeros((DIL_HEAD_DIM - DIL_ROT_DIM,), F32)])
    dil_inv_row = jnp.tile(dil_head, LANES // DIL_HEAD_DIM)[None, :]
    ret_cs, ret_sn, dil_cs, dil_sn = _rope_tables(pos3, ret_inv_row, dil_inv_row)

    mem2 = mem.reshape(b * mem_tokens, d)
    for l in range(depth):
        w_main, w_ab, w_abt = _regroup_w_in(w_in, l, ab0, ab1, tk=128)
        alog_l = _pad_lanes(gdn_a_log[l], GAB_PAD)
        dt_l = _pad_lanes(gdn_dt_bias[l], GAB_PAD)
        alog_r = jnp.repeat(gdn_a_log[l].astype(F32), GDN_CHUNK)[None, :]
        dt_r = jnp.repeat(gdn_dt_bias[l].astype(F32), GDN_CHUNK)[None, :]

        x2 = x.reshape(b * t, d)
        proj, gab, gabt = _in_proj(x2, norm_mix_g[l][None, :], w_main, w_ab, w_abt, tm=512, tn=1280)
        proj3 = proj.reshape(b, t, PROJ_W)
        o_ret = _retention(proj3, ret_cs, ret_sn, ret_norm_g[l])
        n_chunks = b * t // GDN_CHUNK
        gabt_rows = (gabt[:GDN_HEADS].reshape(GDN_HEADS, n_chunks, GDN_CHUNK).transpose(1, 0, 2)
                     .reshape(n_chunks, 1, GDN_HEADS * GDN_CHUNK))
        o_gdn = _gdn(proj3, gab, gabt_rows, gdn_conv_w[l], alog_l, dt_l, alog_r, dt_r,
                     gdn_norm_g[l][None, :])
        o_dil = _dilated(proj3, dil_cs, dil_sn)
        kv = _norm_matmul(mem2, norm_mem_g[l][None, :], xattn_wkv[l].astype(BF16),
                          tm=mem_tokens)
        x3 = _merge_xattn(x, proj3, o_ret, o_gdn, o_dil, kv.reshape(b, mem_tokens, 2 * XATTN_W),
                          w_br_ret[l].astype(BF16), w_br_gdn[l].astype(BF16),
                          w_br_dil[l].astype(BF16), w_out[l].astype(BF16),
                          norm_xattn_g[l][None, :], xattn_wq[l].astype(BF16),
                          xattn_wo[l].astype(BF16), tm=512)

        x = _ffn(x3, norm_ffn_g[l][None, :], ffn_w_up[l].astype(BF16), ffn_conv_w[l],
                 ffn_conv_b[l][None, :], ffn_w_down[l].astype(BF16), final_norm_g[None, :],
                 tm=512, tf=1408, final_norm=(l == depth - 1))
    return x
```

```python
import functools
import math

import jax
import jax.numpy as jnp
import numpy as np
from jax import lax
from jax.experimental import pallas as pl
from jax.experimental.pallas import tpu as pltpu

F32 = jnp.float32
BF16 = jnp.bfloat16

EPS = 1e-6
NEG_INF = -1e30

RET_HEADS, RET_QK_DIM, RET_V_DIM, RET_CHUNK = 4, 64, 128, 128
RET_ROT_BASE = 10000.0
GDN_HEADS, GDN_HEAD_DIM, GDN_CONV, GDN_CHUNK = 4, 128, 4, 64
DIL_GROUPS = ((128, 1), (512, 4), (2048, 16))
DIL_HEADS, DIL_HEAD_DIM = 4, 64
DIL_ROT_DIM = DIL_HEAD_DIM // 4
DIL_SPAN = 128
ROPE_THETA = 500000.0
XATTN_HEADS, XATTN_HEAD_DIM = 4, 128
FFN_CONV = 3
N_BRANCH = 3

RET_QK_W = RET_HEADS * RET_QK_DIM
RET_V_W = RET_HEADS * RET_V_DIM
GDN_W = GDN_HEADS * GDN_HEAD_DIM
DIL_GW = DIL_HEADS * DIL_HEAD_DIM
DIL_W = len(DIL_GROUPS) * DIL_GW
XATTN_W = XATTN_HEADS * XATTN_HEAD_DIM

LANES = 128
SUBLANES = 8
VMEM_LIMIT_BYTES = 56 * 1024 * 1024

COL_GATES = 0
COL_RQ = COL_GATES + N_BRANCH * 1024
COL_RK = COL_RQ + RET_QK_W
COL_RV = COL_RK + RET_QK_W
COL_RG = COL_RV + RET_V_W
COL_GQKV = COL_RG + RET_V_W
COL_GZ = COL_GQKV + 3 * GDN_W
COL_DQ = COL_GZ + GDN_W
COL_DK = COL_DQ + DIL_W
COL_DV = COL_DK + DIL_W
PROJ_W = COL_DV + DIL_W
GAB_PAD = LANES


def _params(*sem):
    return pltpu.CompilerParams(dimension_semantics=sem, vmem_limit_bytes=VMEM_LIMIT_BYTES)


def _rms(x, g):
    return x * lax.rsqrt(jnp.mean(x * x, axis=-1, keepdims=True) + EPS) * g


def _dot(a, b):
    return jnp.dot(a, b, preferred_element_type=F32)


def _dot_nt(a, b):
    return lax.dot_general(a, b, (((1,), (1,)), ((), ())), preferred_element_type=F32)


def _dot_tn(a, b):
    return lax.dot_general(a, b, (((0,), (0,)), ((), ())), preferred_element_type=F32)


def _silu(x):
    return x * jax.nn.sigmoid(x)


def _lane_sum_bcast(x):
    hi = x.astype(BF16)
    lo = (x - hi.astype(F32)).astype(BF16)
    ones = jnp.ones((2 * LANES, LANES), BF16)
    return _dot(jnp.concatenate([hi, lo], axis=1), ones)


def _iota2(shape, dim):
    return lax.broadcasted_iota(jnp.int32, shape, dim)


def _regroup_w_in_kernel(w_ref, o_ref, wab_ref, wabt_ref, *, ab0, ab1):
    n_in = w_ref.shape[2]
    gates0 = n_in - (COL_RQ - COL_GATES)
    o_ref[:, COL_GATES:COL_RQ] = w_ref[0, :, gates0:n_in].astype(BF16)
    o_ref[:, COL_RQ:COL_RQ + ab0] = w_ref[0, :, 0:ab0].astype(BF16)
    o_ref[:, COL_RQ + ab0:PROJ_W] = w_ref[0, :, ab1:gates0].astype(BF16)
    ab = w_ref[0, :, ab0:ab0 + LANES]
    ab = jnp.where(_iota2((1, LANES), 1) < ab1 - ab0, ab, 0.0)
    wab_ref[...] = ab.astype(BF16)
    wabt_ref[...] = ab.T[0:SUBLANES, :].astype(BF16)


def _regroup_w_in(w_in, layer, ab0, ab1, *, tk):
    _, d, n_in = w_in.shape
    assert ab0 % LANES == 0 and ab1 - ab0 <= SUBLANES
    return pl.pallas_call(
        functools.partial(_regroup_w_in_kernel, ab0=ab0, ab1=ab1),
        grid=(d // tk,),
        in_specs=[pl.BlockSpec((1, tk, n_in), lambda i: (layer, i, 0))],
        out_specs=[pl.BlockSpec((tk, PROJ_W), lambda i: (i, 0)),
                   pl.BlockSpec((tk, GAB_PAD), lambda i: (i, 0)),
                   pl.BlockSpec((SUBLANES, tk), lambda i: (0, i))],
        out_shape=[jax.ShapeDtypeStruct((d, PROJ_W), BF16),
                   jax.ShapeDtypeStruct((d, GAB_PAD), BF16),
                   jax.ShapeDtypeStruct((SUBLANES, d), BF16)],
        compiler_params=_params("parallel"),
        name="regroup_w_in",
    )(w_in)


def _in_proj_kernel(x_ref, g_ref, w_ref, wab_ref, wabt_ref, proj_ref, gab_ref, gabt_ref, h_scr, *, tn):
    h_scr[...] = _rms(x_ref[...], g_ref[...]).astype(BF16)
    for j in range(w_ref.shape[1] // tn):
        cols = slice(j * tn, (j + 1) * tn)
        proj_ref[:, cols] = _dot(h_scr[...], w_ref[:, cols]).astype(BF16)
    gab_ref[...] = _dot(h_scr[...], wab_ref[...])
    gabt_ref[...] = _dot_nt(wabt_ref[...], h_scr[...])


def _in_proj(x2, g, w_main, w_ab, w_abt, *, tm, tn):
    m, d = x2.shape
    n = w_main.shape[1]
    resident = lambda a: pl.BlockSpec(a.shape, lambda i: (0, 0), pipeline_mode=pl.Buffered(1))
    return pl.pallas_call(
        functools.partial(_in_proj_kernel, tn=tn),
        grid=(m // tm,),
        in_specs=[
            pl.BlockSpec((tm, d), lambda i: (i, 0)),
            resident(g), resident(w_main), resident(w_ab), resident(w_abt),
        ],
        out_specs=[
            pl.BlockSpec((tm, n), lambda i: (i, 0)),
            pl.BlockSpec((tm, GAB_PAD), lambda i: (i, 0)),
            pl.BlockSpec((SUBLANES, tm), lambda i: (0, i)),
        ],
        out_shape=[
            jax.ShapeDtypeStruct((m, n), BF16),
            jax.ShapeDtypeStruct((m, GAB_PAD), F32),
            jax.ShapeDtypeStruct((SUBLANES, m), F32),
        ],
        scratch_shapes=[pltpu.VMEM((tm, d), BF16)],
        compiler_params=_params("parallel"),
        name="in_proj",
    )(x2, g, w_main, w_ab, w_abt)


def _norm_matmul_kernel(x_ref, g_ref, w_ref, o_ref):
    hb = _rms(x_ref[...], g_ref[...]).astype(BF16)
    o_ref[...] = _dot(hb, w_ref[...]).astype(BF16)


def _norm_matmul(x2, g, w, *, tm):
    m, d = x2.shape
    n = w.shape[1]
    return pl.pallas_call(
        _norm_matmul_kernel,
        grid=(m // tm,),
        in_specs=[
            pl.BlockSpec((tm, d), lambda i: (i, 0)),
            pl.BlockSpec((1, d), lambda i: (0, 0)),
            pl.BlockSpec((d, n), lambda i: (0, 0)),
        ],
        out_specs=pl.BlockSpec((tm, n), lambda i: (i, 0)),
        out_shape=jax.ShapeDtypeStruct((m, n), BF16),
        compiler_params=_params("parallel"),
        name="norm_matmul",
    )(x2, g, w)


def _ret_kernel(cs_ref, sn_ref, q_ref, k_ref, v_ref, gate_ref, ng_ref, o_ref, state):
    t = q_ref.shape[1]
    c = RET_CHUNK
    nh = RET_HEADS
    state[...] = jnp.zeros_like(state)

    lane = _iota2((1, RET_QK_W), 1)
    first_half = (lane % RET_QK_DIM) < (RET_QK_DIM // 2)
    head_of_lane = lane // RET_QK_DIM
    diff = (_iota2((c, c), 0) - _iota2((c, c), 1)).astype(F32)
    idx_col = _iota2((c, 1), 0).astype(F32)
    idx_row = _iota2((1, c), 1).astype(F32)
    log_gamma = [math.log1p(-(2.0 ** (-5.0 - h))) for h in range(nh)]
    decay_all = jnp.concatenate(
        [jnp.where(diff >= 0, jnp.exp(jnp.maximum(diff, 0.0) * lg), 0.0) for lg in log_gamma], axis=0)
    q_decay_all = jnp.concatenate([jnp.exp((idx_col + 1.0) * lg) for lg in log_gamma], axis=0)
    k_decay = [jnp.exp((c - 1.0 - idx_row) * lg) for lg in log_gamma]
    chunk_decay = [math.exp(c * lg) for lg in log_gamma]
    ng_all = jnp.concatenate(
        [jnp.broadcast_to(ng_ref[h:h + 1, :], (c, RET_V_DIM)) for h in range(nh)], axis=0)

    def rope(z, cs, sn):
        partner = jnp.where(first_half, -pltpu.roll(z, RET_QK_W - RET_QK_DIM // 2, 1),
                            pltpu.roll(z, RET_QK_DIM // 2, 1))
        return z * cs + partner * sn

    batch_rows = range(q_ref.shape[0])
    heads = range(nh)
    hrows = [slice(h * c, (h + 1) * c) for h in heads]
    hcols = [slice(h * RET_V_DIM, (h + 1) * RET_V_DIM) for h in heads]

    def chunk(ci, carry):
        r0 = pl.multiple_of(ci * c, c)
        rows = pl.ds(r0, c)
        q_st, k_t = [], []
        for r in batch_rows:
            cs, sn = cs_ref[r, rows, :], sn_ref[r, rows, :]
            cs = jnp.concatenate([cs, cs], axis=1)
            sn = jnp.concatenate([sn, sn], axis=1)
            q = rope(q_ref[r, rows, :].astype(F32), cs, sn)
            k = rope(k_ref[r, rows, :].astype(F32), cs, sn) * (RET_QK_DIM ** -0.5)
            k_t.append(k.T)
            q_st.append(jnp.concatenate([jnp.where(head_of_lane == h, q, 0.0) for h in heads], axis=0))
        s_bf = [(_dot(q_st[r].astype(BF16), k_t[r].astype(BF16)) * decay_all).astype(BF16)
                for r in batch_rows]
        qd_bf = [(q_st[r] * q_decay_all).astype(BF16) for r in batch_rows]
        vs = [[v_ref[r, rows, hcols[h]] for h in heads] for r in batch_rows]
        sts = [[state[r, h] for h in heads] for r in batch_rows]
        intra = [[_dot(s_bf[r][hrows[h]], vs[r][h]) for h in heads] for r in batch_rows]
        inter = [[_dot(qd_bf[r][hrows[h]], sts[r][h].astype(BF16)) for h in heads] for r in batch_rows]
        upd = [[_dot((k_t[r] * k_decay[h]).astype(BF16), vs[r][h]) for h in heads] for r in batch_rows]
        for r in batch_rows:
            for h in heads:
                state[r, h] = sts[r][h] * chunk_decay[h] + upd[r][h]
        for r in batch_rows:
            o_all = jnp.concatenate([intra[r][h] + inter[r][h] for h in heads], axis=0)
            ss = _lane_sum_bcast(o_all * o_all)
            y_all = o_all * lax.rsqrt(ss * (1.0 / RET_V_DIM) + EPS) * ng_all
            for h in heads:
                y = y_all[hrows[h]] * _silu(gate_ref[r, rows, hcols[h]].astype(F32))
                o_ref[r, rows, hcols[h]] = y.astype(BF16)
        return carry

    lax.fori_loop(0, t // c, chunk, 0)


RET_ROWS_PER_STEP = 2


def _retention(proj3, ret_cs, ret_sn, norm_g):
    b, t, _ = proj3.shape
    nr = RET_ROWS_PER_STEP
    rows = lambda width, col=0: pl.BlockSpec((nr, t, width), lambda i: (i, 0, col))
    return pl.pallas_call(
        _ret_kernel,
        grid=(b // nr,),
        in_specs=[
            rows(LANES), rows(LANES),
            rows(RET_QK_W, COL_RQ // RET_QK_W), rows(RET_QK_W, COL_RK // RET_QK_W),
            rows(RET_V_W, COL_RV // RET_V_W), rows(RET_V_W, COL_RG // RET_V_W),
            pl.BlockSpec((RET_HEADS, RET_V_DIM), lambda i: (0, 0)),
        ],
        out_specs=rows(RET_V_W),
        out_shape=jax.ShapeDtypeStruct((b, t, RET_V_W), BF16),
        scratch_shapes=[pltpu.VMEM((nr, RET_HEADS, RET_QK_W, RET_V_DIM), F32)],
        compiler_params=_params("parallel"),
        name="retention",
    )(ret_cs, ret_sn, proj3, proj3, proj3, proj3, norm_g)


GDN_CHUNKS_PER_STEP = 4
GDN_CHUNKS_PER_BLOCK = 8
GDN_ROWS_PER_STEP = 4


def _softplus(x):
    return jnp.maximum(x, 0.0) + jnp.log1p(jnp.exp(-jnp.abs(x)))


def _gdn_prep_kernel(qkv_ref, gab_ref, gabt_ref, cw_ref, alog_l_ref, dt_l_ref, alog_r_ref, dt_r_ref,
                     u_ref, w_ref, qg_ref, kg_ref, attn_ref, egl_ref):
    c = GDN_CHUNK
    hd = GDN_HEAD_DIM
    nh = GDN_HEADS
    m = nh * c
    win = 2 * c
    n_shifted = GDN_CONV - 1
    block0 = pl.program_id(1) * GDN_CHUNKS_PER_BLOCK
    sel_row, sel_col = _iota2((n_shifted * c, win), 0), _iota2((n_shifted * c, win), 1)
    sel_target = (sel_row % c) + (sel_row // c) - n_shifted

    row, col = _iota2((m, m), 0), _iota2((m, m), 1)
    same_head = (row // c) == (col // c)
    tri_bd = same_head & (row >= col)
    strict_bd = same_head & (row > col)
    triu_bd = (same_head & (row <= col)).astype(F32)
    tril_c = (_iota2((c, c), 0) >= _iota2((c, c), 1)).astype(F32)
    scale = hd ** -0.5
    n_doublings = int(math.log2(c)) - 1

    def stack_heads(y, base):
        return jnp.concatenate([y[:, base + h * hd:base + (h + 1) * hd] for h in range(nh)], axis=0)

    def stack_cols(a, first):
        return jnp.concatenate([a[:, first + h:first + h + 1] for h in range(nh)], axis=0)

    def prepare(j):
        ci = block0 + j
        r0 = pl.multiple_of(ci * c, c)
        rows = pl.ds(r0, c)
        start = pl.multiple_of(jnp.maximum(r0 - c, 0), c)
        shift = jnp.where(ci > 0, c, 0)
        sel = (sel_col == sel_target + shift).astype(BF16)
        shifted = _dot(sel, qkv_ref[0, pl.ds(start, win), :])
        y = cw_ref[n_shifted:GDN_CONV, :] * qkv_ref[0, rows, :].astype(F32)
        for tap in range(n_shifted):
            y = y + cw_ref[tap:tap + 1, :] * shifted[tap * c:(tap + 1) * c]
        y = _silu(y)
        q, k, v = stack_heads(y, 0), stack_heads(y, GDN_W), stack_heads(y, 2 * GDN_W)

        gab = gab_ref[rows, :]
        g_col = -jnp.exp(alog_l_ref[...]) * _softplus(gab + dt_l_ref[...])
        gc_col = jnp.dot(tril_c, g_col, precision=lax.Precision.HIGHEST,
                         preferred_element_type=F32)
        beta = stack_cols(jax.nn.sigmoid(gab), nh)
        gcc = stack_cols(gc_col, 0)
        g_last = [gc_col[c - 1:c, h:h + 1] for h in range(nh)]
        g_last_all = jnp.concatenate([jnp.broadcast_to(g, (c, 1)) for g in g_last], axis=0)
        g_row = -jnp.exp(alog_r_ref[...]) * _softplus(gabt_ref[ci] + dt_r_ref[...])
        gcr = jnp.dot(jnp.broadcast_to(g_row, (SUBLANES, m)), triu_bd,
                      precision=lax.Precision.HIGHEST, preferred_element_type=F32)[0:1, :]
        decay = jnp.where(tri_bd, jnp.exp(jnp.where(tri_bd, gcc - gcr, 0.0)), 0.0)

        q = q * lax.rsqrt(_lane_sum_bcast(q * q) + EPS) * scale
        k = k * lax.rsqrt(_lane_sum_bcast(k * k) + EPS)
        k_t_bf = k.T.astype(BF16)
        kb = k * beta
        eg = jnp.exp(gcc)
        raw = _dot(jnp.concatenate([kb, q], axis=0).astype(BF16), k_t_bf)
        n_mat = jnp.where(strict_bd, raw[0:m] * decay, 0.0)
        attn_ref[0, j] = jnp.where(tri_bd, raw[m:2 * m] * decay, 0.0).astype(BF16)
        qg_ref[0, j] = (q * eg).astype(BF16)
        kg_ref[0, j] = (k * jnp.exp(g_last_all - gcc)).astype(BF16)
        egl_ref[0, j] = jnp.concatenate(
            [jnp.broadcast_to(jnp.exp(g), (1, hd)) for g in g_last]
            + [jnp.zeros((SUBLANES - nh, hd), F32)], axis=0)
        return dict(n_pow=n_mat, x=jnp.concatenate([v * beta, kb * eg], axis=1))

    def step(si, carry):
        first = si * GDN_CHUNKS_PER_STEP
        chunks = [prepare(first + j) for j in range(GDN_CHUNKS_PER_STEP)]
        for d in chunks:
            d["x"] = d["x"] - _dot(d["n_pow"].astype(BF16), d["x"].astype(BF16))
        for _ in range(n_doublings):
            for d in chunks:
                p_bf = d["n_pow"].astype(BF16)
                d["n_pow"] = _dot(p_bf, p_bf)
            for d in chunks:
                d["x"] = d["x"] + _dot(d["n_pow"].astype(BF16), d["x"].astype(BF16))
        for j, d in enumerate(chunks):
            u_ref[0, first + j] = d["x"][:, :hd]
            w_ref[0, first + j] = d["x"][:, hd:].astype(BF16)
        return carry

    lax.fori_loop(0, GDN_CHUNKS_PER_BLOCK // GDN_CHUNKS_PER_STEP, step, 0)


def _gdn_rec_kernel(u_ref, w_ref, qg_ref, kg_ref, attn_ref, egl_ref, z_ref, ng_ref, o_ref, state):
    c = GDN_CHUNK
    hd = GDN_HEAD_DIM
    nh = GDN_HEADS
    nr = u_ref.shape[0]
    hrows = [slice(h * c, (h + 1) * c) for h in range(nh)]
    ng_all = jnp.broadcast_to(ng_ref[...], (nh * c, hd))

    @pl.when(pl.program_id(1) == 0)
    def _():
        state[...] = jnp.zeros_like(state)

    def chunk(j, carry):
        rows = pl.ds(pl.multiple_of(j * c, c), c)
        sts = [[state[r, h] for h in range(nh)] for r in range(nr)]
        sts_bf = [[s.astype(BF16) for s in row] for row in sts]
        w_bf = [w_ref[r, j] for r in range(nr)]
        qg_bf = [qg_ref[r, j] for r in range(nr)]
        v_new = [[u_ref[r, j, hrows[h], :] - _dot(w_bf[r][hrows[h]], sts_bf[r][h]) for h in range(nh)]
                 for r in range(nr)]
        o_state = [[_dot(qg_bf[r][hrows[h]], sts_bf[r][h]) for h in range(nh)] for r in range(nr)]
        v_new_bf = [[vn.astype(BF16) for vn in row] for row in v_new]
        for r in range(nr):
            for h in range(nh):
                state[r, h] = (sts[r][h] * egl_ref[r, j, h:h + 1, :]
                               + _dot_tn(kg_ref[r, j, hrows[h], :], v_new_bf[r][h]))
        for r in range(nr):
            o = (jnp.concatenate(o_state[r], axis=0)
                 + _dot(attn_ref[r, j], jnp.concatenate(v_new_bf[r], axis=0)))
            y = o * lax.rsqrt(_lane_sum_bcast(o * o) * (1.0 / hd) + EPS) * ng_all
            for h in range(nh):
                cols = slice(h * hd, (h + 1) * hd)
                o_ref[r, rows, cols] = (y[hrows[h]] * _silu(z_ref[r, rows, cols].astype(F32))).astype(BF16)
        return carry

    lax.fori_loop(0, u_ref.shape[1], chunk, 0)


def _gdn(proj3, gab, gabt_rows, conv_w, alog_l, dt_l, alog_r, dt_r, norm_g):
    b, t, _ = proj3.shape
    c, hd, nh = GDN_CHUNK, GDN_HEAD_DIM, GDN_HEADS
    m = nh * c
    nc = t // c
    cb = GDN_CHUNKS_PER_BLOCK
    per_chunk = lambda width, dtype: jax.ShapeDtypeStruct((b, nc, m, width), dtype)
    chunk_spec = lambda width: pl.BlockSpec((1, cb, m, width), lambda i, j: (i, j, 0, 0))
    u, w, qg, kg, attn, egl = pl.pallas_call(
        _gdn_prep_kernel,
        grid=(b, nc // cb),
        in_specs=[
            pl.BlockSpec((1, t, 3 * GDN_W), lambda i, j: (i, 0, COL_GQKV // (3 * GDN_W))),
            pl.BlockSpec((t, GAB_PAD), lambda i, j: (i, 0)),
            pl.BlockSpec((nc, 1, m), lambda i, j: (i, 0, 0)),
            pl.BlockSpec((GDN_CONV, 3 * GDN_W), lambda i, j: (0, 0)),
            pl.BlockSpec((1, GAB_PAD), lambda i, j: (0, 0)),
            pl.BlockSpec((1, GAB_PAD), lambda i, j: (0, 0)),
            pl.BlockSpec((1, m), lambda i, j: (0, 0)),
            pl.BlockSpec((1, m), lambda i, j: (0, 0)),
        ],
        out_specs=[chunk_spec(hd), chunk_spec(hd), chunk_spec(hd), chunk_spec(hd), chunk_spec(m),
                   pl.BlockSpec((1, cb, SUBLANES, hd), lambda i, j: (i, j, 0, 0))],
        out_shape=[per_chunk(hd, F32), per_chunk(hd, BF16), per_chunk(hd, BF16), per_chunk(hd, BF16),
                   per_chunk(m, BF16), jax.ShapeDtypeStruct((b, nc, SUBLANES, hd), F32)],
        compiler_params=_params("parallel", "arbitrary"),
        name="gdn_prep",
    )(proj3, gab, gabt_rows, conv_w, alog_l, dt_l, alog_r, dt_r)

    nr = GDN_ROWS_PER_STEP
    rec_spec = lambda width: pl.BlockSpec((nr, cb, m, width), lambda i, j: (i, j, 0, 0))
    return pl.pallas_call(
        _gdn_rec_kernel,
        grid=(b // nr, nc // cb),
        in_specs=[rec_spec(hd), rec_spec(hd), rec_spec(hd), rec_spec(hd), rec_spec(m),
                  pl.BlockSpec((nr, cb, SUBLANES, hd), lambda i, j: (i, j, 0, 0)),
                  pl.BlockSpec((nr, cb * c, GDN_W), lambda i, j: (i, j, COL_GZ // GDN_W)),
                  pl.BlockSpec((1, hd), lambda i, j: (0, 0))],
        out_specs=pl.BlockSpec((nr, cb * c, GDN_W), lambda i, j: (i, j, 0)),
        out_shape=jax.ShapeDtypeStruct((b, t, GDN_W), BF16),
        scratch_shapes=[pltpu.VMEM((nr, nh, hd, hd), F32)],
        compiler_params=_params("parallel", "arbitrary"),
        name="gdn_rec",
    )(u, w, qg, kg, attn, egl, proj3, norm_g)


def _rope_tables_kernel(pos_ref, ret_inv_ref, dil_inv_ref, ret_cs_ref, ret_sn_ref, dil_cs_ref,
                        dil_sn_ref):
    pos = pos_ref[0]
    ang = pos * ret_inv_ref[...]
    ret_cs_ref[0] = jnp.cos(ang)
    ret_sn_ref[0] = jnp.sin(ang)
    ang = pos * dil_inv_ref[...]
    dil_cs_ref[0] = jnp.cos(ang)
    dil_sn_ref[0] = jnp.sin(ang)


def _rope_tables(pos3, ret_inv_row, dil_inv_row):
    b, t, _ = pos3.shape
    table = jax.ShapeDtypeStruct((b, t, LANES), F32)
    spec = pl.BlockSpec((1, t, LANES), lambda i: (i, 0, 0))
    return pl.pallas_call(
        _rope_tables_kernel,
        grid=(b,),
        in_specs=[
            pl.BlockSpec((1, t, 1), lambda i: (i, 0, 0)),
            pl.BlockSpec((1, LANES), lambda i: (0, 0)),
            pl.BlockSpec((1, LANES), lambda i: (0, 0)),
        ],
        out_specs=[spec, spec, spec, spec],
        out_shape=[table, table, table, table],
        compiler_params=_params("parallel"),
        name="rope_tables",
    )(pos3, ret_inv_row, dil_inv_row)


DIL_BLOCKS_PER_STEP = 4


def _split_store(ref, lead, rows, val):
    for half in range(val.shape[1] // LANES):
        ref[lead + (half, rows, slice(None))] = val[:, half * LANES:(half + 1) * LANES]


def _split_load(ref, lead, rows):
    return jnp.concatenate([ref[lead + (half, rows, slice(None))] for half in range(DIL_GW // LANES)],
                           axis=1)


def _dil_group(g, dilation, qs, ks, vs, o_scr, lse_scr):
    t = qs.shape[1]
    span = DIL_SPAN
    nh = DIL_HEADS
    nb = t // dilation // span
    nk = 2 * span if nb > 1 else span
    head_of_lane = _iota2((1, DIL_GW), 1) // DIL_HEAD_DIM
    iq = _iota2((nh * span, nk), 0) % span
    ik = _iota2((nh * span, nk), 1)
    if nb > 1:
        dist = span + iq - ik
        band = (dist >= 0) & (dist <= span)
    else:
        band = iq >= ik

    def rows_of(r, n):
        if dilation == 1:
            return pl.ds(pl.multiple_of(span * n, span), span)
        return pl.ds(r + dilation * span * n, span, stride=dilation)

    def scores(r, n):
        cur = rows_of(r, n)
        qb = _split_load(qs, (), cur)
        if nb > 1:
            prev = rows_of(r, jnp.maximum(n - 1, 0) if dilation == 1 else max(n - 1, 0))
            kk = jnp.concatenate([_split_load(ks, (), prev), _split_load(ks, (), cur)], axis=0)
            vv = jnp.concatenate([_split_load(vs, (), prev), _split_load(vs, (), cur)], axis=0)
            valid = band & (ik >= jnp.where(n > 0, 0, span))
        else:
            kk, vv, valid = _split_load(ks, (), cur), _split_load(vs, (), cur), band
        q_st = jnp.concatenate([jnp.where(head_of_lane == h, qb, 0.0) for h in range(nh)], axis=0)
        s = _dot_nt(q_st.astype(BF16), kk.astype(BF16))
        s = jnp.where(valid, s, NEG_INF)
        m = jnp.max(s, axis=-1, keepdims=True)
        p = jnp.exp(s - m)
        den = jnp.sum(p, axis=-1, keepdims=True)
        pn = (p * (1.0 / den)).astype(BF16)
        return cur, pn, vv, m + jnp.log(den)

    def outputs(cur, pn, vv, lse):
        p_cat = jnp.concatenate([pn[h * span:(h + 1) * span] for h in range(nh)], axis=1)
        v_st = jnp.concatenate([jnp.where(head_of_lane == h, vv, 0.0) for h in range(nh)],
                               axis=0).astype(BF16)
        lse_x = jnp.zeros((span, DIL_GW), F32)
        for h in range(nh):
            lse_x = jnp.where(head_of_lane == h, lse[h * span:(h + 1) * span], lse_x)
        _split_store(o_scr, (g,), cur, _dot(p_cat, v_st))
        _split_store(lse_scr, (g,), cur, lse_x)

    def run(blocks):
        staged = [scores(r, n) for r, n in blocks]
        for args in staged:
            outputs(*args)

    step = min(DIL_BLOCKS_PER_STEP, nb * dilation)
    if dilation == 1:
        def body(i, carry):
            run([(0, i * step + j) for j in range(step)])
            return carry
        lax.fori_loop(0, nb // step, body, 0)
    else:
        blocks = [(r, n) for r in range(dilation) for n in range(nb)]
        for i in range(0, len(blocks), step):
            run(blocks[i:i + step])


def _dil_kernel(cs_ref, sn_ref, q_ref, k_ref, v_ref, o_ref, s1_scr, s2_scr, qs, ks, vs, o_scr, lse_scr):
    g = pl.program_id(1)
    half = DIL_ROT_DIM // 2

    @pl.when(g == 0)
    def _():
        lane = _iota2((1, LANES), 1) % DIL_HEAD_DIM
        sn = sn_ref[0]
        s1_scr[...] = jnp.where(lane < half, -sn, 0.0)
        s2_scr[...] = jnp.where((lane >= half) & (lane < DIL_ROT_DIM), sn, 0.0)

    def both(a):
        return jnp.concatenate([a, a], axis=1)

    def rope(z):
        return (z * both(cs_ref[0]) + pltpu.roll(z, DIL_GW - half, 1) * both(s1_scr[...])
                + pltpu.roll(z, half, 1) * both(s2_scr[...]))

    everything = slice(None)
    _split_store(qs, (), everything, rope(q_ref[0].astype(F32)) * (DIL_HEAD_DIM ** -0.5))
    _split_store(ks, (), everything, rope(k_ref[0].astype(F32)))
    _split_store(vs, (), everything, v_ref[0].astype(F32))

    for gi, (window, dilation) in enumerate(DIL_GROUPS):
        assert window // dilation == DIL_SPAN

        @pl.when(g == gi)
        def _(gi=gi, dilation=dilation):
            _dil_group(gi, dilation, qs, ks, vs, o_scr, lse_scr)

    @pl.when(g == len(DIL_GROUPS) - 1)
    def _():
        for half_i in range(DIL_GW // LANES):
            lses = [lse_scr[i, half_i] for i in range(len(DIL_GROUPS))]
            m = functools.reduce(jnp.maximum, lses)
            es = [jnp.exp(l - m) for l in lses]
            den = functools.reduce(lambda a, b: a + b, es)
            num = functools.reduce(lambda a, b: a + b,
                                   [e * o_scr[i, half_i] for i, e in enumerate(es)])
            o_ref[0, :, half_i * LANES:(half_i + 1) * LANES] = (num / den).astype(BF16)


def _dilated(proj3, dil_cs, dil_sn):
    b, t, _ = proj3.shape
    ng = len(DIL_GROUPS)
    table = pltpu.VMEM((t, LANES), F32)
    split = pltpu.VMEM((DIL_GW // LANES, t, LANES), F32)
    return pl.pallas_call(
        _dil_kernel,
        grid=(b, ng),
        in_specs=[
            pl.BlockSpec((1, t, LANES), lambda i, g: (i, 0, 0)),
            pl.BlockSpec((1, t, LANES), lambda i, g: (i, 0, 0)),
            pl.BlockSpec((1, t, DIL_GW), lambda i, g: (i, 0, COL_DQ // DIL_GW + g)),
            pl.BlockSpec((1, t, DIL_GW), lambda i, g: (i, 0, COL_DK // DIL_GW + g)),
            pl.BlockSpec((1, t, DIL_GW), lambda i, g: (i, 0, COL_DV // DIL_GW + g)),
        ],
        out_specs=pl.BlockSpec((1, t, DIL_GW), lambda i, g: (i, 0, 0)),
        out_shape=jax.ShapeDtypeStruct((b, t, DIL_GW), BF16),
        scratch_shapes=[table, table, split, split, split,
                        pltpu.VMEM((ng, DIL_GW // LANES, t, LANES), F32),
                        pltpu.VMEM((ng, DIL_GW // LANES, t, LANES), F32)],
        compiler_params=_params("parallel", "arbitrary"),
        name="dilated",
    )(dil_cs, dil_sn, proj3, proj3, proj3)


def _merge_xattn_kernel(x_ref, gates_ref, oret_ref, ogdn_ref, odil_ref, kv_ref, wr_ref, wg_ref,
                        wd_ref, wo_ref, g_ref, wq_ref, wxo_ref, out_ref):
    d = x_ref.shape[2]
    branches = (_dot(oret_ref[0], wr_ref[...]), _dot(ogdn_ref[0], wg_ref[...]),
                _dot(odil_ref[0], wd_ref[...]))
    merged = None
    for bi, br in enumerate(branches):
        term = jax.nn.sigmoid(gates_ref[0, :, bi * d:(bi + 1) * d].astype(F32)) * br
        merged = term if merged is None else merged + term
    y = x_ref[0] + _dot(merged.astype(BF16), wo_ref[...])

    q = _dot(_rms(y, g_ref[...]).astype(BF16), wq_ref[...])
    outs = []
    for h in range(XATTN_HEADS):
        cols = slice(h * XATTN_HEAD_DIM, (h + 1) * XATTN_HEAD_DIM)
        kh = kv_ref[0, :, cols]
        vh = kv_ref[0, :, XATTN_W + h * XATTN_HEAD_DIM:XATTN_W + (h + 1) * XATTN_HEAD_DIM]
        s = _dot_nt(q[:, cols].astype(BF16), kh) * (XATTN_HEAD_DIM ** -0.5)
        e = jnp.exp(s - jnp.max(s, axis=-1, keepdims=True))
        p = e / jnp.sum(e, axis=-1, keepdims=True)
        outs.append(_dot(p.astype(BF16), vh))
    o = jnp.concatenate(outs, axis=1).astype(BF16)
    out_ref[0] = y + _dot(o, wxo_ref[...])


def _merge_xattn(x3, proj3, o_ret, o_gdn, o_dil, kv3, w_ret, w_gdn, w_dil, w_out, g, wq, wxo, *, tm):
    b, t, d = x3.shape
    mem = kv3.shape[1]
    rows = lambda width, col=0: pl.BlockSpec((1, tm, width), lambda i, j: (i, j, col))
    resident = lambda a: pl.BlockSpec(a.shape, lambda i, j: (0, 0), pipeline_mode=pl.Buffered(1))
    return pl.pallas_call(
        _merge_xattn_kernel,
        grid=(b, t // tm),
        in_specs=[
            rows(d), rows(N_BRANCH * d, COL_GATES // (N_BRANCH * d)), rows(RET_V_W), rows(GDN_W),
            rows(DIL_GW),
            pl.BlockSpec((1, mem, 2 * XATTN_W), lambda i, j: (i, 0, 0)),
            resident(w_ret), resident(w_gdn), resident(w_dil), resident(w_out), resident(g),
            resident(wq), resident(wxo),
        ],
        out_specs=rows(d),
        out_shape=jax.ShapeDtypeStruct((b, t, d), F32),
        compiler_params=_params("parallel", "parallel"),
        name="merge_xattn",
    )(x3, proj3, o_ret, o_gdn, o_dil, kv3, w_ret, w_gdn, w_dil, w_out, g, wq, wxo)


def _ffn_kernel(x_ref, g_ref, wup_ref, cw_ref, cb_ref, wd_ref, fg_ref, out_ref,
                h_scr, act_scr, a_scr, u_scr, hist_a, hist_u, *, tf, final_norm):
    tm = x_ref.shape[1]
    ffn_dim = wd_ref.shape[0]
    halo = SUBLANES

    @pl.when(pl.program_id(1) == 0)
    def _():
        hist_a[...] = jnp.zeros_like(hist_a)
        hist_u[...] = jnp.zeros_like(hist_u)

    x = x_ref[0]
    h_scr[...] = _rms(x, g_ref[...]).astype(BF16)

    def up_proj(scr, hist, c, col0):
        up = _dot(h_scr[...], wup_ref[:, col0:col0 + tf])
        scr[c % 2, 0:halo, :] = hist[c]
        scr[c % 2, halo:, :] = up
        hist[c] = up[tm - halo:tm]

    def conv(scr, c, col0):
        cols = slice(col0, col0 + tf)
        groups = tm // SUBLANES
        cur = scr[c % 2, halo:halo + tm, :].reshape(groups, SUBLANES, tf)
        prv = scr[c % 2, 0:tm, :].reshape(groups, SUBLANES, tf)
        sub = _iota2((1, SUBLANES, 1), 1)
        acc = cb_ref[:, cols] + cw_ref[FFN_CONV - 1:FFN_CONV, cols] * cur
        for back in range(1, FFN_CONV):
            shifted = jnp.where(sub >= back, pltpu.roll(cur, back, 1), pltpu.roll(prv, back, 1))
            acc = acc + cw_ref[FFN_CONV - 1 - back:FFN_CONV - back, cols] * shifted
        return acc.reshape(tm, tf)

    nf = ffn_dim // tf
    up_proj(a_scr, hist_a, 0, 0)
    up_proj(u_scr, hist_u, 0, ffn_dim)
    for c in range(nf):
        if c + 1 < nf:
            up_proj(a_scr, hist_a, c + 1, (c + 1) * tf)
            up_proj(u_scr, hist_u, c + 1, ffn_dim + (c + 1) * tf)
        act = _silu(conv(a_scr, c, c * tf)) * conv(u_scr, c, ffn_dim + c * tf)
        act_scr[:, c * tf:(c + 1) * tf] = act.astype(BF16)

    out = x + _dot(act_scr[...], wd_ref[...])
    if final_norm:
        out = _rms(out, fg_ref[...])
    out_ref[0] = out


def _ffn(x3, g, w_up, conv_w, conv_b, w_down, final_g, *, tm, tf, final_norm):
    b, t, d = x3.shape
    ffn_dim = w_down.shape[0]
    nf = ffn_dim // tf
    resident = lambda a: pl.BlockSpec(a.shape, lambda i, j: (0, 0), pipeline_mode=pl.Buffered(1))
    return pl.pallas_call(
        functools.partial(_ffn_kernel, tf=tf, final_norm=final_norm),
        grid=(b, t // tm),
        in_specs=[
            pl.BlockSpec((1, tm, d), lambda i, j: (i, j, 0)),
            resident(g), resident(w_up), resident(conv_w), resident(conv_b), resident(w_down),
            resident(final_g),
        ],
        out_specs=pl.BlockSpec((1, tm, d), lambda i, j: (i, j, 0)),
        out_shape=jax.ShapeDtypeStruct((b, t, d), F32),
        scratch_shapes=[
            pltpu.VMEM((tm, d), BF16),
            pltpu.VMEM((tm, ffn_dim), BF16),
            pltpu.VMEM((2, tm + SUBLANES, tf), F32),
            pltpu.VMEM((2, tm + SUBLANES, tf), F32),
            pltpu.VMEM((nf, SUBLANES, tf), F32),
            pltpu.VMEM((nf, SUBLANES, tf), F32),
        ],
        compiler_params=_params("parallel", "arbitrary"),
        name="ffn",
    )(x3, g, w_up, conv_w, conv_b, w_down, final_g)


def _pad_lanes(v, width):
    return jnp.pad(v.astype(F32), (0, width - v.shape[0]))[None, :]


def kernel(x, mem, positions, norm_mix_g, w_in, ret_norm_g, gdn_conv_w, gdn_a_log, gdn_dt_bias,
           gdn_norm_g, w_br_ret, w_br_gdn, w_br_dil, w_out, norm_xattn_g, norm_mem_g, xattn_wq,
           xattn_wkv, xattn_wo, norm_ffn_g, ffn_w_up, ffn_conv_w, ffn_conv_b, ffn_w_down,
           final_norm_g):
    b, t, d = x.shape
    depth = w_in.shape[0]
    mem_tokens = mem.shape[1]
    ab0 = 2 * RET_QK_W + 2 * RET_V_W + 3 * GDN_W
    ab1 = ab0 + 2 * GDN_HEADS

    pos3 = positions.astype(F32)[:, :, None]
    ret_inv = 1.0 / (RET_ROT_BASE ** jnp.linspace(0.0, 1.0, RET_QK_DIM // 2, dtype=F32))
    ret_inv_row = jnp.tile(ret_inv, LANES // ret_inv.shape[0])[None, :]
    dil_inv = ROPE_THETA ** (-jnp.arange(0, DIL_ROT_DIM, 2, dtype=F32) / DIL_ROT_DIM)
    dil_head = jnp.concatenate([dil_inv, dil_inv, jnp.zeros((DIL_HEAD_DIM - DIL_ROT_DIM,), F32)])
    dil_inv_row = jnp.tile(dil_head, LANES // DIL_HEAD_DIM)[None, :]
    ret_cs, ret_sn, dil_cs, dil_sn = _rope_tables(pos3, ret_inv_row, dil_inv_row)

    mem2 = mem.reshape(b * mem_tokens, d)
    for l in range(depth):
        w_main, w_ab, w_abt = _regroup_w_in(w_in, l, ab0, ab1, tk=128)
        alog_l = _pad_lanes(gdn_a_log[l], GAB_PAD)
        dt_l = _pad_lanes(gdn_dt_bias[l], GAB_PAD)
        alog_r = jnp.repeat(gdn_a_log[l].astype(F32), GDN_CHUNK)[None, :]
        dt_r = jnp.repeat(gdn_dt_bias[l].astype(F32), GDN_CHUNK)[None, :]

        x2 = x.reshape(b * t, d)
        proj, gab, gabt = _in_proj(x2, norm_mix_g[l][None, :], w_main, w_ab, w_abt, tm=512, tn=1280)
        proj3 = proj.reshape(b, t, PROJ_W)
        o_ret = _retention(proj3, ret_cs, ret_sn, ret_norm_g[l])
        n_chunks = b * t // GDN_CHUNK
        gabt_rows = (gabt[:GDN_HEADS].reshape(GDN_HEADS, n_chunks, GDN_CHUNK).transpose(1, 0, 2)
                     .reshape(n_chunks, 1, GDN_HEADS * GDN_CHUNK))
        o_gdn = _gdn(proj3, gab, gabt_rows, gdn_conv_w[l], alog_l, dt_l, alog_r, dt_r,
                     gdn_norm_g[l][None, :])
        o_dil = _dilated(proj3, dil_cs, dil_sn)
        kv = _norm_matmul(mem2, norm_mem_g[l][None, :], xattn_wkv[l].astype(BF16),
                          tm=mem_tokens)
        x3 = _merge_xattn(x, proj3, o_ret, o_gdn, o_dil, kv.reshape(b, mem_tokens, 2 * XATTN_W),
                          w_br_ret[l].astype(BF16), w_br_gdn[l].astype(BF16),
                          w_br_dil[l].astype(BF16), w_out[l].astype(BF16),
                          norm_xattn_g[l][None, :], xattn_wq[l].astype(BF16),
                          xattn_wo[l].astype(BF16), tm=512)

        x = _ffn(x3, norm_ffn_g[l][None, :], ffn_w_up[l].astype(BF16), ffn_conv_w[l],
                 ffn_conv_b[l][None, :], ffn_w_down[l].astype(BF16), final_norm_g[None, :],
                 tm=512, tf=1408, final_norm=(l == depth - 1))
    return x
```

```python
import functools
import math

import jax
import jax.numpy as jnp
from jax import lax
from jax.experimental import pallas as pl
from jax.experimental.pallas import tpu as pltpu

F32 = jnp.float32
BF16 = jnp.bfloat16

EPS = 1e-6
NEG_INF = -1e30

RET_HEADS, RET_QK_DIM, RET_V_DIM, RET_CHUNK = 4, 64, 128, 128
RET_ROT_BASE = 10000.0
GDN_HEADS, GDN_HEAD_DIM, GDN_CONV, GDN_CHUNK = 4, 128, 4, 64
DIL_GROUPS = ((128, 1), (512, 4), (2048, 16))
DIL_HEADS, DIL_HEAD_DIM = 4, 64
DIL_ROT_DIM = DIL_HEAD_DIM // 4
DIL_SPAN = 128
ROPE_THETA = 500000.0
XATTN_HEADS, XATTN_HEAD_DIM = 4, 128
FFN_CONV = 3
N_BRANCH = 3

RET_QK_W = RET_HEADS * RET_QK_DIM
RET_V_W = RET_HEADS * RET_V_DIM
GDN_W = GDN_HEADS * GDN_HEAD_DIM
DIL_GW = DIL_HEADS * DIL_HEAD_DIM
DIL_W = len(DIL_GROUPS) * DIL_GW
XATTN_W = XATTN_HEADS * XATTN_HEAD_DIM

LANES = 128
SUBLANES = 8
VMEM_LIMIT_BYTES = 56 * 1024 * 1024

COL_GATES = 0
COL_RQ = COL_GATES + N_BRANCH * 1024
COL_RK = COL_RQ + RET_QK_W
COL_RV = COL_RK + RET_QK_W
COL_RG = COL_RV + RET_V_W
COL_GQKV = COL_RG + RET_V_W
COL_GZ = COL_GQKV + 3 * GDN_W
COL_DQ = COL_GZ + GDN_W
COL_DK = COL_DQ + DIL_W
COL_DV = COL_DK + DIL_W
PROJ_W = COL_DV + DIL_W
GAB_PAD = LANES


def _params(*sem):
    return pltpu.CompilerParams(dimension_semantics=sem, vmem_limit_bytes=VMEM_LIMIT_BYTES)


def _rms(x, g):
    return x * lax.rsqrt(jnp.mean(x * x, axis=-1, keepdims=True) + EPS) * g


def _dot(a, b):
    return jnp.dot(a, b, preferred_element_type=F32)


def _dot_nt(a, b):
    return lax.dot_general(a, b, (((1,), (1,)), ((), ())), preferred_element_type=F32)


def _dot_tn(a, b):
    return lax.dot_general(a, b, (((0,), (0,)), ((), ())), preferred_element_type=F32)


def _silu(x):
    return x * jax.nn.sigmoid(x)


def _lane_sum_bcast(x):
    hi = x.astype(BF16)
    lo = (x - hi.astype(F32)).astype(BF16)
    ones = jnp.ones((2 * LANES, LANES), BF16)
    return _dot(jnp.concatenate([hi, lo], axis=1), ones)


def _iota2(shape, dim):
    return lax.broadcasted_iota(jnp.int32, shape, dim)


def _regroup_w_in_kernel(w_ref, o_ref, wab_ref, wabt_ref, *, ab0, ab1):
    n_in = w_ref.shape[2]
    gates0 = n_in - (COL_RQ - COL_GATES)
    o_ref[:, COL_GATES:COL_RQ] = w_ref[0, :, gates0:n_in].astype(BF16)
    o_ref[:, COL_RQ:COL_RQ + ab0] = w_ref[0, :, 0:ab0].astype(BF16)
    o_ref[:, COL_RQ + ab0:PROJ_W] = w_ref[0, :, ab1:gates0].astype(BF16)
    ab = w_ref[0, :, ab0:ab0 + LANES]
    ab = jnp.where(_iota2((1, LANES), 1) < ab1 - ab0, ab, 0.0)
    wab_ref[...] = ab.astype(BF16)
    wabt_ref[...] = ab.T[0:SUBLANES, :].astype(BF16)


def _regroup_w_in(w_in, layer, ab0, ab1, *, tk):
    _, d, n_in = w_in.shape
    assert ab0 % LANES == 0 and ab1 - ab0 <= SUBLANES
    return pl.pallas_call(
        functools.partial(_regroup_w_in_kernel, ab0=ab0, ab1=ab1),
        grid=(d // tk,),
        in_specs=[pl.BlockSpec((1, tk, n_in), lambda i: (layer, i, 0))],
        out_specs=[pl.BlockSpec((tk, PROJ_W), lambda i: (i, 0)),
                   pl.BlockSpec((tk, GAB_PAD), lambda i: (i, 0)),
                   pl.BlockSpec((SUBLANES, tk), lambda i: (0, i))],
        out_shape=[jax.ShapeDtypeStruct((d, PROJ_W), BF16),
                   jax.ShapeDtypeStruct((d, GAB_PAD), BF16),
                   jax.ShapeDtypeStruct((SUBLANES, d), BF16)],
        compiler_params=_params("parallel"),
        name="regroup_w_in",
    )(w_in)


def _in_proj_kernel(x_ref, g_ref, w_ref, wab_ref, wabt_ref, proj_ref, gab_ref, gabt_ref, h_scr, *, tn):
    h_scr[...] = _rms(x_ref[...], g_ref[...]).astype(BF16)
    for j in range(w_ref.shape[1] // tn):
        cols = slice(j * tn, (j + 1) * tn)
        proj_ref[:, cols] = _dot(h_scr[...], w_ref[:, cols]).astype(BF16)
    gab_ref[...] = _dot(h_scr[...], wab_ref[...])
    gabt_ref[...] = _dot_nt(wabt_ref[...], h_scr[...])


def _in_proj(x2, g, w_main, w_ab, w_abt, *, tm, tn):
    m, d = x2.shape
    n = w_main.shape[1]
    resident = lambda a: pl.BlockSpec(a.shape, lambda i: (0, 0), pipeline_mode=pl.Buffered(1))
    return pl.pallas_call(
        functools.partial(_in_proj_kernel, tn=tn),
        grid=(m // tm,),
        in_specs=[
            pl.BlockSpec((tm, d), lambda i: (i, 0)),
            resident(g), resident(w_main), resident(w_ab), resident(w_abt),
        ],
        out_specs=[
            pl.BlockSpec((tm, n), lambda i: (i, 0)),
            pl.BlockSpec((tm, GAB_PAD), lambda i: (i, 0)),
            pl.BlockSpec((SUBLANES, tm), lambda i: (0, i)),
        ],
        out_shape=[
            jax.ShapeDtypeStruct((m, n), BF16),
            jax.ShapeDtypeStruct((m, GAB_PAD), F32),
            jax.ShapeDtypeStruct((SUBLANES, m), F32),
        ],
        scratch_shapes=[pltpu.VMEM((tm, d), BF16)],
        compiler_params=_params("parallel"),
        name="in_proj",
    )(x2, g, w_main, w_ab, w_abt)


def _norm_matmul_kernel(x_ref, g_ref, w_ref, o_ref):
    hb = _rms(x_ref[...], g_ref[...]).astype(BF16)
    o_ref[...] = _dot(hb, w_ref[...]).astype(BF16)


def _norm_matmul(x2, g, w, *, tm):
    m, d = x2.shape
    n = w.shape[1]
    return pl.pallas_call(
        _norm_matmul_kernel,
        grid=(m // tm,),
        in_specs=[
            pl.BlockSpec((tm, d), lambda i: (i, 0)),
            pl.BlockSpec((1, d), lambda i: (0, 0)),
            pl.BlockSpec((d, n), lambda i: (0, 0)),
        ],
        out_specs=pl.BlockSpec((tm, n), lambda i: (i, 0)),
        out_shape=jax.ShapeDtypeStruct((m, n), BF16),
        compiler_params=_params("parallel"),
        name="norm_matmul",
    )(x2, g, w)


def _ret_kernel(cs_ref, sn_ref, q_ref, k_ref, v_ref, gate_ref, ng_ref, o_ref, state):
    t = q_ref.shape[1]
    c = RET_CHUNK
    nh = RET_HEADS
    state[...] = jnp.zeros_like(state)

    lane = _iota2((1, RET_QK_W), 1)
    first_half = (lane % RET_QK_DIM) < (RET_QK_DIM // 2)
    head_of_lane = lane // RET_QK_DIM
    diff = (_iota2((c, c), 0) - _iota2((c, c), 1)).astype(F32)
    idx_col = _iota2((c, 1), 0).astype(F32)
    idx_row = _iota2((1, c), 1).astype(F32)
    log_gamma = [math.log1p(-(2.0 ** (-5.0 - h))) for h in range(nh)]
    decay_all = jnp.concatenate(
        [jnp.where(diff >= 0, jnp.exp(jnp.maximum(diff, 0.0) * lg), 0.0) for lg in log_gamma], axis=0)
    q_decay_all = jnp.concatenate([jnp.exp((idx_col + 1.0) * lg) for lg in log_gamma], axis=0)
    k_decay = [jnp.exp((c - 1.0 - idx_row) * lg) for lg in log_gamma]
    chunk_decay = [math.exp(c * lg) for lg in log_gamma]
    ng_all = jnp.concatenate(
        [jnp.broadcast_to(ng_ref[h:h + 1, :], (c, RET_V_DIM)) for h in range(nh)], axis=0)

    def rope(z, cs, sn):
        partner = jnp.where(first_half, -pltpu.roll(z, RET_QK_W - RET_QK_DIM // 2, 1),
                            pltpu.roll(z, RET_QK_DIM // 2, 1))
        return z * cs + partner * sn

    batch_rows = range(q_ref.shape[0])
    heads = range(nh)
    hrows = [slice(h * c, (h + 1) * c) for h in heads]
    hcols = [slice(h * RET_V_DIM, (h + 1) * RET_V_DIM) for h in heads]

    def chunk(ci, carry):
        r0 = pl.multiple_of(ci * c, c)
        rows = pl.ds(r0, c)
        q_st, k_t = [], []
        for r in batch_rows:
            cs, sn = cs_ref[r, rows, :], sn_ref[r, rows, :]
            cs = jnp.concatenate([cs, cs], axis=1)
            sn = jnp.concatenate([sn, sn], axis=1)
            q = rope(q_ref[r, rows, :].astype(F32), cs, sn)
            k = rope(k_ref[r, rows, :].astype(F32), cs, sn) * (RET_QK_DIM ** -0.5)
            k_t.append(k.T)
            q_st.append(jnp.concatenate([jnp.where(head_of_lane == h, q, 0.0) for h in heads], axis=0))
        s_bf = [(_dot(q_st[r].astype(BF16), k_t[r].astype(BF16)) * decay_all).astype(BF16)
                for r in batch_rows]
        qd_bf = [(q_st[r] * q_decay_all).astype(BF16) for r in batch_rows]
        vs = [[v_ref[r, rows, hcols[h]] for h in heads] for r in batch_rows]
        sts = [[state[r, h] for h in heads] for r in batch_rows]
        intra = [[_dot(s_bf[r][hrows[h]], vs[r][h]) for h in heads] for r in batch_rows]
        inter = [[_dot(qd_bf[r][hrows[h]], sts[r][h].astype(BF16)) for h in heads] for r in batch_rows]
        upd = [[_dot((k_t[r] * k_decay[h]).astype(BF16), vs[r][h]) for h in heads] for r in batch_rows]
        for r in batch_rows:
            for h in heads:
                state[r, h] = sts[r][h] * chunk_decay[h] + upd[r][h]
        for r in batch_rows:
            o_all = jnp.concatenate([intra[r][h] + inter[r][h] for h in heads], axis=0)
            ss = _lane_sum_bcast(o_all * o_all)
            y_all = o_all * lax.rsqrt(ss * (1.0 / RET_V_DIM) + EPS) * ng_all
            for h in heads:
                y = y_all[hrows[h]] * _silu(gate_ref[r, rows, hcols[h]].astype(F32))
                o_ref[r, rows, hcols[h]] = y.astype(BF16)
        return carry

    lax.fori_loop(0, t // c, chunk, 0)


RET_ROWS_PER_STEP = 2


def _retention(proj3, ret_cs, ret_sn, norm_g):
    b, t, _ = proj3.shape
    nr = RET_ROWS_PER_STEP
    rows = lambda width, col=0: pl.BlockSpec((nr, t, width), lambda i: (i, 0, col))
    return pl.pallas_call(
        _ret_kernel,
        grid=(b // nr,),
        in_specs=[
            rows(LANES), rows(LANES),
            rows(RET_QK_W, COL_RQ // RET_QK_W), rows(RET_QK_W, COL_RK // RET_QK_W),
            rows(RET_V_W, COL_RV // RET_V_W), rows(RET_V_W, COL_RG // RET_V_W),
            pl.BlockSpec((RET_HEADS, RET_V_DIM), lambda i: (0, 0)),
        ],
        out_specs=rows(RET_V_W),
        out_shape=jax.ShapeDtypeStruct((b, t, RET_V_W), BF16),
        scratch_shapes=[pltpu.VMEM((nr, RET_HEADS, RET_QK_W, RET_V_DIM), F32)],
        compiler_params=_params("parallel"),
        name="retention",
    )(ret_cs, ret_sn, proj3, proj3, proj3, proj3, norm_g)


GDN_CHUNKS_PER_STEP = 4
GDN_CHUNKS_PER_BLOCK = 8
GDN_ROWS_PER_STEP = 4


def _softplus(x):
    return jnp.maximum(x, 0.0) + jnp.log1p(jnp.exp(-jnp.abs(x)))


def _gdn_prep_kernel(qkv_ref, gab_ref, gabt_ref, cw_ref, alog_l_ref, dt_l_ref, alog_r_ref, dt_r_ref,
                     u_ref, w_ref, qg_ref, kg_ref, attn_ref, egl_ref):
    c = GDN_CHUNK
    hd = GDN_HEAD_DIM
    nh = GDN_HEADS
    m = nh * c
    win = 2 * c
    n_shifted = GDN_CONV - 1
    block0 = pl.program_id(1) * GDN_CHUNKS_PER_BLOCK
    sel_row, sel_col = _iota2((n_shifted * c, win), 0), _iota2((n_shifted * c, win), 1)
    sel_target = (sel_row % c) + (sel_row // c) - n_shifted

    row, col = _iota2((m, m), 0), _iota2((m, m), 1)
    same_head = (row // c) == (col // c)
    tri_bd = same_head & (row >= col)
    strict_bd = same_head & (row > col)
    triu_bd = (same_head & (row <= col)).astype(F32)
    tril_c = (_iota2((c, c), 0) >= _iota2((c, c), 1)).astype(F32)
    scale = hd ** -0.5
    n_doublings = int(math.log2(c)) - 1

    def stack_heads(y, base):
        return jnp.concatenate([y[:, base + h * hd:base + (h + 1) * hd] for h in range(nh)], axis=0)

    def stack_cols(a, first):
        return jnp.concatenate([a[:, first + h:first + h + 1] for h in range(nh)], axis=0)

    def prepare(j):
        ci = block0 + j
        r0 = pl.multiple_of(ci * c, c)
        rows = pl.ds(r0, c)
        start = pl.multiple_of(jnp.maximum(r0 - c, 0), c)
        shift = jnp.where(ci > 0, c, 0)
        sel = (sel_col == sel_target + shift).astype(BF16)
        shifted = _dot(sel, qkv_ref[0, pl.ds(start, win), :])
        y = cw_ref[n_shifted:GDN_CONV, :] * qkv_ref[0, rows, :].astype(F32)
        for tap in range(n_shifted):
            y = y + cw_ref[tap:tap + 1, :] * shifted[tap * c:(tap + 1) * c]
        y = _silu(y)
        q, k, v = stack_heads(y, 0), stack_heads(y, GDN_W), stack_heads(y, 2 * GDN_W)

        gab = gab_ref[rows, :]
        g_col = -jnp.exp(alog_l_ref[...]) * _softplus(gab + dt_l_ref[...])
        gc_col = jnp.dot(tril_c, g_col, precision=lax.Precision.HIGHEST,
                         preferred_element_type=F32)
        beta = stack_cols(jax.nn.sigmoid(gab), nh)
        gcc = stack_cols(gc_col, 0)
        g_last = [gc_col[c - 1:c, h:h + 1] for h in range(nh)]
        g_last_all = jnp.concatenate([jnp.broadcast_to(g, (c, 1)) for g in g_last], axis=0)
        g_row = -jnp.exp(alog_r_ref[...]) * _softplus(gabt_ref[ci] + dt_r_ref[...])
        gcr = jnp.dot(jnp.broadcast_to(g_row, (SUBLANES, m)), triu_bd,
                      precision=lax.Precision.HIGHEST, preferred_element_type=F32)[0:1, :]
        decay = jnp.where(tri_bd, jnp.exp(jnp.where(tri_bd, gcc - gcr, 0.0)), 0.0)

        q = q * lax.rsqrt(_lane_sum_bcast(q * q) + EPS) * scale
        k = k * lax.rsqrt(_lane_sum_bcast(k * k) + EPS)
        k_t_bf = k.T.astype(BF16)
        kb = k * beta
        eg = jnp.exp(gcc)
        raw = _dot(jnp.concatenate([kb, q], axis=0).astype(BF16), k_t_bf)
        n_mat = jnp.where(strict_bd, raw[0:m] * decay, 0.0)
        attn_ref[0, j] = jnp.where(tri_bd, raw[m:2 * m] * decay, 0.0).astype(BF16)
        qg_ref[0, j] = (q * eg).astype(BF16)
        kg_ref[0, j] = (k * jnp.exp(g_last_all - gcc)).astype(BF16)
        egl_ref[0, j] = jnp.concatenate(
            [jnp.broadcast_to(jnp.exp(g), (1, hd)) for g in g_last]
            + [jnp.zeros((SUBLANES - nh, hd), F32)], axis=0)
        return dict(n_pow=n_mat, x=jnp.concatenate([v * beta, kb * eg], axis=1))

    def step(si, carry):
        first = si * GDN_CHUNKS_PER_STEP
        chunks = [prepare(first + j) for j in range(GDN_CHUNKS_PER_STEP)]
        for d in chunks:
            d["x"] = d["x"] - _dot(d["n_pow"].astype(BF16), d["x"].astype(BF16))
        for _ in range(n_doublings):
            for d in chunks:
                p_bf = d["n_pow"].astype(BF16)
                d["n_pow"] = _dot(p_bf, p_bf)
            for d in chunks:
                d["x"] = d["x"] + _dot(d["n_pow"].astype(BF16), d["x"].astype(BF16))
        for j, d in enumerate(chunks):
            u_ref[0, first + j] = d["x"][:, :hd]
            w_ref[0, first + j] = d["x"][:, hd:].astype(BF16)
        return carry

    lax.fori_loop(0, GDN_CHUNKS_PER_BLOCK // GDN_CHUNKS_PER_STEP, step, 0)


def _gdn_rec_kernel(u_ref, w_ref, qg_ref, kg_ref, attn_ref, egl_ref, z_ref, ng_ref, o_ref, state):
    c = GDN_CHUNK
    hd = GDN_HEAD_DIM
    nh = GDN_HEADS
    nr = u_ref.shape[0]
    hrows = [slice(h * c, (h + 1) * c) for h in range(nh)]
    ng_all = jnp.broadcast_to(ng_ref[...], (nh * c, hd))

    @pl.when(pl.program_id(1) == 0)
    def _():
        state[...] = jnp.zeros_like(state)

    def chunk(j, carry):
        rows = pl.ds(pl.multiple_of(j * c, c), c)
        sts = [[state[r, h] for h in range(nh)] for r in range(nr)]
        sts_bf = [[s.astype(BF16) for s in row] for row in sts]
        w_bf = [w_ref[r, j] for r in range(nr)]
        qg_bf = [qg_ref[r, j] for r in range(nr)]
        v_new = [[u_ref[r, j, hrows[h], :] - _dot(w_bf[r][hrows[h]], sts_bf[r][h]) for h in range(nh)]
                 for r in range(nr)]
        o_state = [[_dot(qg_bf[r][hrows[h]], sts_bf[r][h]) for h in range(nh)] for r in range(nr)]
        v_new_bf = [[vn.astype(BF16) for vn in row] for row in v_new]
        for r in range(nr):
            for h in range(nh):
                state[r, h] = (sts[r][h] * egl_ref[r, j, h:h + 1, :]
                               + _dot_tn(kg_ref[r, j, hrows[h], :], v_new_bf[r][h]))
        for r in range(nr):
            o = (jnp.concatenate(o_state[r], axis=0)
                 + _dot(attn_ref[r, j], jnp.concatenate(v_new_bf[r], axis=0)))
            y = o * lax.rsqrt(_lane_sum_bcast(o * o) * (1.0 / hd) + EPS) * ng_all
            for h in range(nh):
                cols = slice(h * hd, (h + 1) * hd)
                o_ref[r, rows, cols] = (y[hrows[h]] * _silu(z_ref[r, rows, cols].astype(F32))).astype(BF16)
        return carry

    lax.fori_loop(0, u_ref.shape[1], chunk, 0)


def _gdn(proj3, gab, gabt_rows, conv_w, alog_l, dt_l, alog_r, dt_r, norm_g):
    b, t, _ = proj3.shape
    c, hd, nh = GDN_CHUNK, GDN_HEAD_DIM, GDN_HEADS
    m = nh * c
    nc = t // c
    cb = GDN_CHUNKS_PER_BLOCK
    per_chunk = lambda width, dtype: jax.ShapeDtypeStruct((b, nc, m, width), dtype)
    chunk_spec = lambda width: pl.BlockSpec((1, cb, m, width), lambda i, j: (i, j, 0, 0))
    u, w, qg, kg, attn, egl = pl.pallas_call(
        _gdn_prep_kernel,
        grid=(b, nc // cb),
        in_specs=[
            pl.BlockSpec((1, t, 3 * GDN_W), lambda i, j: (i, 0, COL_GQKV // (3 * GDN_W))),
            pl.BlockSpec((t, GAB_PAD), lambda i, j: (i, 0)),
            pl.BlockSpec((nc, 1, m), lambda i, j: (i, 0, 0)),
            pl.BlockSpec((GDN_CONV, 3 * GDN_W), lambda i, j: (0, 0)),
            pl.BlockSpec((1, GAB_PAD), lambda i, j: (0, 0)),
            pl.BlockSpec((1, GAB_PAD), lambda i, j: (0, 0)),
            pl.BlockSpec((1, m), lambda i, j: (0, 0)),
            pl.BlockSpec((1, m), lambda i, j: (0, 0)),
        ],
        out_specs=[chunk_spec(hd), chunk_spec(hd), chunk_spec(hd), chunk_spec(hd), chunk_spec(m),
                   pl.BlockSpec((1, cb, SUBLANES, hd), lambda i, j: (i, j, 0, 0))],
        out_shape=[per_chunk(hd, F32), per_chunk(hd, BF16), per_chunk(hd, BF16), per_chunk(hd, BF16),
                   per_chunk(m, BF16), jax.ShapeDtypeStruct((b, nc, SUBLANES, hd), F32)],
        compiler_params=_params("parallel", "arbitrary"),
        name="gdn_prep",
    )(proj3, gab, gabt_rows, conv_w, alog_l, dt_l, alog_r, dt_r)

    nr = GDN_ROWS_PER_STEP
    rec_spec = lambda width: pl.BlockSpec((nr, cb, m, width), lambda i, j: (i, j, 0, 0))
    return pl.pallas_call(
        _gdn_rec_kernel,
        grid=(b // nr, nc // cb),
        in_specs=[rec_spec(hd), rec_spec(hd), rec_spec(hd), rec_spec(hd), rec_spec(m),
                  pl.BlockSpec((nr, cb, SUBLANES, hd), lambda i, j: (i, j, 0, 0)),
                  pl.BlockSpec((nr, cb * c, GDN_W), lambda i, j: (i, j, COL_GZ // GDN_W)),
                  pl.BlockSpec((1, hd), lambda i, j: (0, 0))],
        out_specs=pl.BlockSpec((nr, cb * c, GDN_W), lambda i, j: (i, j, 0)),
        out_shape=jax.ShapeDtypeStruct((b, t, GDN_W), BF16),
        scratch_shapes=[pltpu.VMEM((nr, nh, hd, hd), F32)],
        compiler_params=_params("parallel", "arbitrary"),
        name="gdn_rec",
    )(u, w, qg, kg, attn, egl, proj3, norm_g)


def _rope_tables_kernel(pos_ref, ret_inv_ref, dil_inv_ref, ret_cs_ref, ret_sn_ref, dil_cs_ref,
                        dil_sn_ref):
    pos = pos_ref[0]
    ang = pos * ret_inv_ref[...]
    ret_cs_ref[0] = jnp.cos(ang)
    ret_sn_ref[0] = jnp.sin(ang)
    ang = pos * dil_inv_ref[...]
    dil_cs_ref[0] = jnp.cos(ang)
    dil_sn_ref[0] = jnp.sin(ang)


def _rope_tables(pos3, ret_inv_row, dil_inv_row):
    b, t, _ = pos3.shape
    table = jax.ShapeDtypeStruct((b, t, LANES), F32)
    spec = pl.BlockSpec((1, t, LANES), lambda i: (i, 0, 0))
    return pl.pallas_call(
        _rope_tables_kernel,
        grid=(b,),
        in_specs=[
            pl.BlockSpec((1, t, 1), lambda i: (i, 0, 0)),
            pl.BlockSpec((1, LANES), lambda i: (0, 0)),
            pl.BlockSpec((1, LANES), lambda i: (0, 0)),
        ],
        out_specs=[spec, spec, spec, spec],
        out_shape=[table, table, table, table],
        compiler_params=_params("parallel"),
        name="rope_tables",
    )(pos3, ret_inv_row, dil_inv_row)


DIL_BLOCKS_PER_STEP = 4


def _split_store(ref, lead, rows, val):
    for half in range(val.shape[1] // LANES):
        ref[lead + (half, rows, slice(None))] = val[:, half * LANES:(half + 1) * LANES]


def _split_load(ref, lead, rows):
    return jnp.concatenate([ref[lead + (half, rows, slice(None))] for half in range(DIL_GW // LANES)],
                           axis=1)


def _dil_group(g, dilation, qs, ks, vs, o_scr, lse_scr):
    t = qs.shape[1]
    span = DIL_SPAN
    nh = DIL_HEADS
    nb = t // dilation // span
    nk = 2 * span if nb > 1 else span
    head_of_lane = _iota2((1, DIL_GW), 1) // DIL_HEAD_DIM
    iq = _iota2((nh * span, nk), 0) % span
    ik = _iota2((nh * span, nk), 1)
    if nb > 1:
        dist = span + iq - ik
        band = (dist >= 0) & (dist <= span)
    else:
        band = iq >= ik

    def rows_of(r, n):
        if dilation == 1:
            return pl.ds(pl.multiple_of(span * n, span), span)
        return pl.ds(r + dilation * span * n, span, stride=dilation)

    def scores(r, n):
        cur = rows_of(r, n)
        qb = _split_load(qs, (), cur)
        if nb > 1:
            prev = rows_of(r, jnp.maximum(n - 1, 0) if dilation == 1 else max(n - 1, 0))
            kk = jnp.concatenate([_split_load(ks, (), prev), _split_load(ks, (), cur)], axis=0)
            vv = jnp.concatenate([_split_load(vs, (), prev), _split_load(vs, (), cur)], axis=0)
            valid = band & (ik >= jnp.where(n > 0, 0, span))
        else:
            kk, vv, valid = _split_load(ks, (), cur), _split_load(vs, (), cur), band
        q_st = jnp.concatenate([jnp.where(head_of_lane == h, qb, 0.0) for h in range(nh)], axis=0)
        s = _dot_nt(q_st.astype(BF16), kk.astype(BF16))
        s = jnp.where(valid, s, NEG_INF)
        m = jnp.max(s, axis=-1, keepdims=True)
        p = jnp.exp(s - m)
        den = jnp.sum(p, axis=-1, keepdims=True)
        pn = (p * (1.0 / den)).astype(BF16)
        return cur, pn, vv, m + jnp.log(den)

    def outputs(cur, pn, vv, lse):
        p_cat = jnp.concatenate([pn[h * span:(h + 1) * span] for h in range(nh)], axis=1)
        v_st = jnp.concatenate([jnp.where(head_of_lane == h, vv, 0.0) for h in range(nh)],
                               axis=0).astype(BF16)
        lse_x = jnp.zeros((span, DIL_GW), F32)
        for h in range(nh):
            lse_x = jnp.where(head_of_lane == h, lse[h * span:(h + 1) * span], lse_x)
        _split_store(o_scr, (g,), cur, _dot(p_cat, v_st))
        _split_store(lse_scr, (g,), cur, lse_x)

    def run(blocks):
        staged = [scores(r, n) for r, n in blocks]
        for args in staged:
            outputs(*args)

    step = min(DIL_BLOCKS_PER_STEP, nb * dilation)
    if dilation == 1:
        def body(i, carry):
            run([(0, i * step + j) for j in range(step)])
            return carry
        lax.fori_loop(0, nb // step, body, 0)
    else:
        blocks = [(r, n) for r in range(dilation) for n in range(nb)]
        for i in range(0, len(blocks), step):
            run(blocks[i:i + step])


def _dil_kernel(cs_ref, sn_ref, q_ref, k_ref, v_ref, o_ref, s1_scr, s2_scr, qs, ks, vs, o_scr, lse_scr):
    g = pl.program_id(1)
    half = DIL_ROT_DIM // 2

    @pl.when(g == 0)
    def _():
        lane = _iota2((1, LANES), 1) % DIL_HEAD_DIM
        sn = sn_ref[0]
        s1_scr[...] = jnp.where(lane < half, -sn, 0.0)
        s2_scr[...] = jnp.where((lane >= half) & (lane < DIL_ROT_DIM), sn, 0.0)

    def both(a):
        return jnp.concatenate([a, a], axis=1)

    def rope(z):
        return (z * both(cs_ref[0]) + pltpu.roll(z, DIL_GW - half, 1) * both(s1_scr[...])
                + pltpu.roll(z, half, 1) * both(s2_scr[...]))

    everything = slice(None)
    _split_store(qs, (), everything, rope(q_ref[0].astype(F32)) * (DIL_HEAD_DIM ** -0.5))
    _split_store(ks, (), everything, rope(k_ref[0].astype(F32)))
    _split_store(vs, (), everything, v_ref[0].astype(F32))

    for gi, (window, dilation) in enumerate(DIL_GROUPS):
        assert window // dilation == DIL_SPAN

        @pl.when(g == gi)
        def _(gi=gi, dilation=dilation):
            _dil_group(gi, dilation, qs, ks, vs, o_scr, lse_scr)

    @pl.when(g == len(DIL_GROUPS) - 1)
    def _():
        for half_i in range(DIL_GW // LANES):
            lses = [lse_scr[i, half_i] for i in range(len(DIL_GROUPS))]
            m = functools.reduce(jnp.maximum, lses)
            es = [jnp.exp(l - m) for l in lses]
            den = functools.reduce(lambda a, b: a + b, es)
            num = functools.reduce(lambda a, b: a + b,
                                   [e * o_scr[i, half_i] for i, e in enumerate(es)])
            o_ref[0, :, half_i * LANES:(half_i + 1) * LANES] = (num / den).astype(BF16)


def _dilated(proj3, dil_cs, dil_sn):
    b, t, _ = proj3.shape
    ng = len(DIL_GROUPS)
    table = pltpu.VMEM((t, LANES), F32)
    split = pltpu.VMEM((DIL_GW // LANES, t, LANES), F32)
    return pl.pallas_call(
        _dil_kernel,
        grid=(b, ng),
        in_specs=[
            pl.BlockSpec((1, t, LANES), lambda i, g: (i, 0, 0)),
            pl.BlockSpec((1, t, LANES), lambda i, g: (i, 0, 0)),
            pl.BlockSpec((1, t, DIL_GW), lambda i, g: (i, 0, COL_DQ // DIL_GW + g)),
            pl.BlockSpec((1, t, DIL_GW), lambda i, g: (i, 0, COL_DK // DIL_GW + g)),
            pl.BlockSpec((1, t, DIL_GW), lambda i, g: (i, 0, COL_DV // DIL_GW + g)),
        ],
        out_specs=pl.BlockSpec((1, t, DIL_GW), lambda i, g: (i, 0, 0)),
        out_shape=jax.ShapeDtypeStruct((b, t, DIL_GW), BF16),
        scratch_shapes=[table, table, split, split, split,
                        pltpu.VMEM((ng, DIL_GW // LANES, t, LANES), F32),
                        pltpu.VMEM((ng, DIL_GW // LANES, t, LANES), F32)],
        compiler_params=_params("parallel", "arbitrary"),
        name="dilated",
    )(dil_cs, dil_sn, proj3, proj3, proj3)


def _merge_xattn_kernel(x_ref, gates_ref, oret_ref, ogdn_ref, odil_ref, kv_ref, wr_ref, wg_ref,
                        wd_ref, wo_ref, g_ref, wq_ref, wxo_ref, out_ref):
    d = x_ref.shape[2]
    branches = (_dot(oret_ref[0], wr_ref[...]), _dot(ogdn_ref[0], wg_ref[...]),
                _dot(odil_ref[0], wd_ref[...]))
    merged = None
    for bi, br in enumerate(branches):
        term = jax.nn.sigmoid(gates_ref[0, :, bi * d:(bi + 1) * d].astype(F32)) * br
        merged = term if merged is None else merged + term
    y = x_ref[0] + _dot(merged.astype(BF16), wo_ref[...])

    q = _dot(_rms(y, g_ref[...]).astype(BF16), wq_ref[...])
    outs = []
    for h in range(XATTN_HEADS):
        cols = slice(h * XATTN_HEAD_DIM, (h + 1) * XATTN_HEAD_DIM)
        kh = kv_ref[0, :, cols]
        vh = kv_ref[0, :, XATTN_W + h * XATTN_HEAD_DIM:XATTN_W + (h + 1) * XATTN_HEAD_DIM]
        s = _dot_nt(q[:, cols].astype(BF16), kh) * (XATTN_HEAD_DIM ** -0.5)
        e = jnp.exp(s - jnp.max(s, axis=-1, keepdims=True))
        p = e / jnp.sum(e, axis=-1, keepdims=True)
        outs.append(_dot(p.astype(BF16), vh))
    o = jnp.concatenate(outs, axis=1).astype(BF16)
    out_ref[0] = y + _dot(o, wxo_ref[...])


def _merge_xattn(x3, proj3, o_ret, o_gdn, o_dil, kv3, w_ret, w_gdn, w_dil, w_out, g, wq, wxo, *, tm):
    b, t, d = x3.shape
    mem = kv3.shape[1]
    rows = lambda width, col=0: pl.BlockSpec((1, tm, width), lambda i, j: (i, j, col))
    resident = lambda a: pl.BlockSpec(a.shape, lambda i, j: (0, 0), pipeline_mode=pl.Buffered(1))
    return pl.pallas_call(
        _merge_xattn_kernel,
        grid=(b, t // tm),
        in_specs=[
            rows(d), rows(N_BRANCH * d, COL_GATES // (N_BRANCH * d)), rows(RET_V_W), rows(GDN_W),
            rows(DIL_GW),
            pl.BlockSpec((1, mem, 2 * XATTN_W), lambda i, j: (i, 0, 0)),
            resident(w_ret), resident(w_gdn), resident(w_dil), resident(w_out), resident(g),
            resident(wq), resident(wxo),
        ],
        out_specs=rows(d),
        out_shape=jax.ShapeDtypeStruct((b, t, d), F32),
        compiler_params=_params("parallel", "parallel"),
        name="merge_xattn",
    )(x3, proj3, o_ret, o_gdn, o_dil, kv3, w_ret, w_gdn, w_dil, w_out, g, wq, wxo)


def _ffn_kernel(x_ref, g_ref, wup_ref, cw_ref, cb_ref, wd_ref, fg_ref, out_ref,
                h_scr, act_scr, a_scr, u_scr, hist_a, hist_u, *, tf, final_norm):
    tm = x_ref.shape[1]
    ffn_dim = wd_ref.shape[0]
    halo = SUBLANES

    @pl.when(pl.program_id(1) == 0)
    def _():
        hist_a[...] = jnp.zeros_like(hist_a)
        hist_u[...] = jnp.zeros_like(hist_u)

    x = x_ref[0]
    h_scr[...] = _rms(x, g_ref[...]).astype(BF16)

    def up_proj(scr, hist, c, col0):
        up = _dot(h_scr[...], wup_ref[:, col0:col0 + tf])
        scr[c % 2, 0:halo, :] = hist[c]
        scr[c % 2, halo:, :] = up
        hist[c] = up[tm - halo:tm]

    def conv(scr, c, col0):
        cols = slice(col0, col0 + tf)
        acc = cb_ref[:, cols]
        for j in range(FFN_CONV):
            off = halo - (FFN_CONV - 1) + j
            acc = acc + cw_ref[j:j + 1, cols] * scr[c % 2, off:off + tm, :]
        return acc

    nf = ffn_dim // tf
    up_proj(a_scr, hist_a, 0, 0)
    up_proj(u_scr, hist_u, 0, ffn_dim)
    for c in range(nf):
        if c + 1 < nf:
            up_proj(a_scr, hist_a, c + 1, (c + 1) * tf)
            up_proj(u_scr, hist_u, c + 1, ffn_dim + (c + 1) * tf)
        act = _silu(conv(a_scr, c, c * tf)) * conv(u_scr, c, ffn_dim + c * tf)
        act_scr[:, c * tf:(c + 1) * tf] = act.astype(BF16)

    out = x + _dot(act_scr[...], wd_ref[...])
    if final_norm:
        out = _rms(out, fg_ref[...])
    out_ref[0] = out


def _ffn(x3, g, w_up, conv_w, conv_b, w_down, final_g, *, tm, tf, final_norm):
    b, t, d = x3.shape
    ffn_dim = w_down.shape[0]
    nf = ffn_dim // tf
    resident = lambda a: pl.BlockSpec(a.shape, lambda i, j: (0, 0), pipeline_mode=pl.Buffered(1))
    return pl.pallas_call(
        functools.partial(_ffn_kernel, tf=tf, final_norm=final_norm),
        grid=(b, t // tm),
        in_specs=[
            pl.BlockSpec((1, tm, d), lambda i, j: (i, j, 0)),
            resident(g), resident(w_up), resident(conv_w), resident(conv_b), resident(w_down),
            resident(final_g),
        ],
        out_specs=pl.BlockSpec((1, tm, d), lambda i, j: (i, j, 0)),
        out_shape=jax.ShapeDtypeStruct((b, t, d), F32),
        scratch_shapes=[
            pltpu.VMEM((tm, d), BF16),
            pltpu.VMEM((tm, ffn_dim), BF16),
            pltpu.VMEM((2, tm + SUBLANES, tf), F32),
            pltpu.VMEM((2, tm + SUBLANES, tf), F32),
            pltpu.VMEM((nf, SUBLANES, tf), F32),
            pltpu.VMEM((nf, SUBLANES, tf), F32),
        ],
        compiler_params=_params("parallel", "arbitrary"),
        name="ffn",
    )(x3, g, w_up, conv_w, conv_b, w_down, final_g)


def _pad_lanes(v, width):
    return jnp.pad(v.astype(F32), (0, width - v.shape[0]))[None, :]


def kernel(x, mem, positions, norm_mix_g, w_in, ret_norm_g, gdn_conv_w, gdn_a_log, gdn_dt_bias,
           gdn_norm_g, w_br_ret, w_br_gdn, w_br_dil, w_out, norm_xattn_g, norm_mem_g, xattn_wq,
           xattn_wkv, xattn_wo, norm_ffn_g, ffn_w_up, ffn_conv_w, ffn_conv_b, ffn_w_down,
           final_norm_g):
    b, t, d = x.shape
    depth = w_in.shape[0]
    mem_tokens = mem.shape[1]
    ab0 = 2 * RET_QK_W + 2 * RET_V_W + 3 * GDN_W
    ab1 = ab0 + 2 * GDN_HEADS

    pos3 = positions.astype(F32)[:, :, None]
    ret_inv = 1.0 / (RET_ROT_BASE ** jnp.linspace(0.0, 1.0, RET_QK_DIM // 2, dtype=F32))
    ret_inv_row = jnp.tile(ret_inv, LANES // ret_inv.shape[0])[None, :]
    dil_inv = ROPE_THETA ** (-jnp.arange(0, DIL_ROT_DIM, 2, dtype=F32) / DIL_ROT_DIM)
    dil_head = jnp.concatenate([dil_inv, dil_inv, jnp.zeros((DIL_HEAD_DIM - DIL_ROT_DIM,), F32)])
    dil_inv_row = jnp.tile(dil_head, LANES // DIL_HEAD_DIM)[None, :]
    ret_cs, ret_sn, dil_cs, dil_sn = _rope_tables(pos3, ret_inv_row, dil_inv_row)

    mem2 = mem.reshape(b * mem_tokens, d)
    for l in range(depth):
        w_main, w_ab, w_abt = _regroup_w_in(w_in, l, ab0, ab1, tk=128)
        alog_l = _pad_lanes(gdn_a_log[l], GAB_PAD)
        dt_l = _pad_lanes(gdn_dt_bias[l], GAB_PAD)
        alog_r = jnp.repeat(gdn_a_log[l].astype(F32), GDN_CHUNK)[None, :]
        dt_r = jnp.repeat(gdn_dt_bias[l].astype(F32), GDN_CHUNK)[None, :]

        x2 = x.reshape(b * t, d)
        proj, gab, gabt = _in_proj(x2, norm_mix_g[l][None, :], w_main, w_ab, w_abt, tm=512, tn=1280)
        proj3 = proj.reshape(b, t, PROJ_W)
        o_ret = _retention(proj3, ret_cs, ret_sn, ret_norm_g[l])
        n_chunks = b * t // GDN_CHUNK
        gabt_rows = (gabt[:GDN_HEADS].reshape(GDN_HEADS, n_chunks, GDN_CHUNK).transpose(1, 0, 2)
                     .reshape(n_chunks, 1, GDN_HEADS * GDN_CHUNK))
        o_gdn = _gdn(proj3, gab, gabt_rows, gdn_conv_w[l], alog_l, dt_l, alog_r, dt_r,
                     gdn_norm_g[l][None, :])
        o_dil = _dilated(proj3, dil_cs, dil_sn)
        kv = _norm_matmul(mem2, norm_mem_g[l][None, :], xattn_wkv[l].astype(BF16),
                          tm=mem_tokens)
        x3 = _merge_xattn(x, proj3, o_ret, o_gdn, o_dil, kv.reshape(b, mem_tokens, 2 * XATTN_W),
                          w_br_ret[l].astype(BF16), w_br_gdn[l].astype(BF16),
                          w_br_dil[l].astype(BF16), w_out[l].astype(BF16),
                          norm_xattn_g[l][None, :], xattn_wq[l].astype(BF16),
                          xattn_wo[l].astype(BF16), tm=512)

        x = _ffn(x3, norm_ffn_g[l][None, :], ffn_w_up[l].astype(BF16), ffn_conv_w[l],
                 ffn_conv_b[l][None, :], ffn_w_down[l].astype(BF16), final_norm_g[None, :],
                 tm=512, tf=1408, final_norm=(l == depth - 1))
    return x
```
